```python
import math
import jax, jax.numpy as jnp
from jax import lax
import numpy as np

D_MODEL = 1024
BATCH = 8
SEQ = 4096
DEPTH = 1

N_META = 16
D_CONV = D_MODEL
CONV_WIDTH = 31
N_HEADS = 8
HEAD_DIM = 128
N_KV_HEADS = 2
N_IDX_HEADS = 8
IDX_DIM = 64
TOPK_MAX = 256
D_FF = 2816
FFN_CONV_WIDTH = 3
ROPE_THETA = 10000.0
Q_BLOCK = 128
EPS = 1e-6
META_BONUS = 1e30

COLS = (
    2 * D_CONV,
    N_HEADS * HEAD_DIM,
    N_KV_HEADS * HEAD_DIM,
    N_KV_HEADS * HEAD_DIM,
    N_IDX_HEADS * IDX_DIM,
    IDX_DIM,
    N_IDX_HEADS,
    2 * D_MODEL,
)
D_IN_PROJ = sum(COLS)

kernel_name = "hybrid_conformer_dsa_gated_block"


def rms_norm(x, g):
    xf = x.astype(jnp.float32)
    y = xf * lax.rsqrt(jnp.mean(xf * xf, axis=-1, keepdims=True) + EPS)
    return (y * g.astype(jnp.float32)).astype(x.dtype)


def layer_norm(x, g, b):
    xf = x.astype(jnp.float32)
    mu = jnp.mean(xf, axis=-1, keepdims=True)
    var = jnp.mean(jnp.square(xf - mu), axis=-1, keepdims=True)
    y = (xf - mu) * lax.rsqrt(var + EPS)
    return (y * g.astype(jnp.float32) + b.astype(jnp.float32)).astype(x.dtype)


def rope(x, pos):
    half = x.shape[-1] // 2
    inv = ROPE_THETA ** (-jnp.arange(half, dtype=jnp.float32) / half)
    ang = pos.astype(jnp.float32)[:, None] * inv[None, :]
    cos = jnp.cos(ang)[None, :, None, :]
    sin = jnp.sin(ang)[None, :, None, :]
    xf = x.astype(jnp.float32)
    x1, x2 = xf[..., :half], xf[..., half:]
    return jnp.concatenate([x1 * cos - x2 * sin, x2 * cos + x1 * sin], axis=-1).astype(x.dtype)


def causal_depthwise_conv(x, w, b):
    width, c = w.shape
    y = lax.conv_general_dilated(
        x, w[:, None, :].astype(x.dtype), window_strides=(1,),
        padding=[(width - 1, 0)], dimension_numbers=("NWC", "WIO", "NWC"),
        feature_group_count=c)
    return y + b.astype(x.dtype)


def conformer_conv_branch(u, ln_g, ln_b, dw_w, dw_b, pw_out):
    a, gate = jnp.split(u, 2, axis=-1)
    h = a * jax.nn.sigmoid(gate)
    h = causal_depthwise_conv(h, dw_w, dw_b)
    h = layer_norm(h, ln_g, ln_b)
    h = jax.nn.silu(h)
    return h @ pw_out


def dsa_sparse_attention(q, k, v, q_idx, k_idx, w_idx, w_o):
    bsz, seq_len = q.shape[0], q.shape[1]
    top_k = min(TOPK_MAX, seq_len // 4)
    n_blk = -(-seq_len // Q_BLOCK)
    pad = n_blk * Q_BLOCK - seq_len
    rep = N_HEADS // N_KV_HEADS

    def to_blocks(t):
        t = jnp.pad(t, [(0, 0), (0, pad)] + [(0, 0)] * (t.ndim - 2))
        t = t.reshape((bsz, n_blk, Q_BLOCK) + t.shape[2:])
        return jnp.moveaxis(t, 1, 0)

    key_pos = jnp.arange(seq_len, dtype=jnp.int32)
    q_pos_blocks = jnp.arange(n_blk * Q_BLOCK, dtype=jnp.int32).reshape(n_blk, Q_BLOCK)
    k_idx_f = k_idx.astype(jnp.float32)
    gather = jax.vmap(lambda kb, ib: kb[ib])

    def one_block(args):
        qb, qib, wb, qpos = args
        dots = jnp.einsum("bqhd,bsd->bqhs", qib.astype(jnp.float32), k_idx_f) * (IDX_DIM ** -0.5)
        score = jnp.einsum("bqh,bqhs->bqs", wb.astype(jnp.float32) * (N_IDX_HEADS ** -0.5),
                           jax.nn.relu(dots))
        visible = key_pos[None, :] <= qpos[:, None]
        is_meta = visible & (key_pos[None, :] < N_META)
        score = jnp.where(visible[None], score, -jnp.inf)
        score = jnp.where(is_meta[None], jnp.float32(METAB_PLACEHOLDER) if False else jnp.float32(METAB), score) if False else jnp.where(is_meta[None], jnp.float32(META_BONUS), score)
        vals, idx = lax.top_k(score, top_k)
        valid = jnp.isfinite(vals)
        ks = gather(k, idx)
        vs = gather(v, idx)
        qg = qb.reshape(bsz, Q_BLOCK, N_KV_HEADS, rep, HEAD_DIM)
        logits = jnp.einsum("bqgrd,bqkgd->bqgrk", qg, ks).astype(jnp.float32) * (HEAD_DIM ** -0.5)
        logits = jnp.where(valid[:, :, None, None, :], logits, -jnp.inf)
        p = jax.nn.softmax(logits, axis=-1).astype(v.dtype)
        o = jnp.einsum("bqgrk,bqkgd->bqgrd", p, vs)
        return o.reshape(bsz, Q_BLOCK, N_HEADS * HEAD_DIM)

    outs = lax.map(one_block, (to_blocks(q), to_blocks(q_idx), to_blocks(w_idx), q_pos_blocks))
    outs = jnp.moveaxis(outs, 0, 1).reshape(bsz, n_blk * Q_BLOCK, N_HEADS * HEAD_DIM)[:, :seq_len]
    return outs @ w_o


def setup_inputs(seed: int = 0) -> dict:
    key = jax.random.key(seed)
    ks = jax.random.split(key, 20)
    f32 = jnp.float32

    def nrm(k, shape, scale):
        return jax.random.normal(k, shape, f32) * scale

    def gain(k, shape):
        return 1.0 + 0.05 * jax.random.normal(k, shape, f32)

    return {
        "x": jax.random.normal(ks[0], (BATCH, SEQ, D_MODEL), f32),
        "meta_tokens": nrm(ks[1], (N_META, D_MODEL), 1.0),
        "mix_norm_g": gain(ks[2], (DEPTH, D_MODEL)),
        "w_in": nrm(ks[3], (DEPTH, D_MODEL, D_IN_PROJ), D_MODEL ** -0.5),
        "conv_ln_g": gain(ks[4], (DEPTH, D_CONV)),
        "conv_ln_b": nrm(ks[5], (DEPTH, D_CONV), 0.01),
        "conv_dw_w": nrm(ks[6], (DEPTH, CONV_WIDTH, D_CONV), CONV_WIDTH ** -0.5),
        "conv_dw_b": nrm(ks[7], (DEPTH, D_CONV), 0.01),
        "conv_pw_out": nrm(ks[8], (DEPTH, D_CONV, D_MODEL), D_CONV ** -0.5),
        "attn_w_o": nrm(ks[9], (DEPTH, N_HEADS * HEAD_DIM, D_MODEL), (N_HEADS * HEAD_DIM) ** -0.5),
        "w_merge_out": nrm(ks[10], (DEPTH, D_MODEL, D_MODEL), D_MODEL ** -0.5),
        "ffn_norm_g": gain(ks[11], (DEPTH, D_MODEL)),
        "ffn_up": nrm(ks[12], (DEPTH, D_MODEL, 2 * D_FF), D_MODEL ** -0.5),
        "ffn_dw_w": nrm(ks[13], (DEPTH, FFN_CONV_WIDTH, 2 * D_FF), FFN_CONV_WIDTH ** -0.5),
        "ffn_dw_b": nrm(ks[14], (DEPTH, 2 * D_FF), 0.01),
        "ffn_down": nrm(ks[15], (DEPTH, D_FF, D_MODEL), D_FF ** -0.5),
        "final_norm_g": gain(ks[16], (D_MODEL,)),
    }


def reference(x, meta_tokens, mix_norm_g, w_in, conv_ln_g, conv_ln_b, conv_dw_w, conv_dw_b,
              conv_pw_out, attn_w_o, w_merge_out, ffn_norm_g, ffn_up, ffn_dw_w, ffn_dw_b,
              ffn_down, final_norm_g):
    bsz = x.shape[0]
    meta = jnp.broadcast_to(meta_tokens.astype(x.dtype)[None], (bsz, N_META, x.shape[-1]))
    stream = jnp.concatenate([meta, x], axis=1)
    seq_len = stream.shape[1]
    pos = jnp.arange(seq_len, dtype=jnp.int32)
    split_at = np.cumsum(COLS)[:-1].tolist()

    for l in range(DEPTH):
        h = rms_norm(stream, mix_norm_g[l])
        proj = h @ w_in[l]
        u_conv, q, k, v, q_idx, k_idx, w_idx, gates = jnp.split(proj, split_at, axis=-1)
        q = rope(q.reshape(bsz, seq_len, N_HEADS, HEAD_DIM), pos)
        k = rope(k.reshape(bsz, seq_len, N_KV_HEADS, HEAD_DIM), pos)
        v = v.reshape(bsz, seq_len, N_KV_HEADS, HEAD_DIM)
        q_idx = rope(q_idx.reshape(bsz, seq_len, N_IDX_HEADS, IDX_DIM), pos)
        k_idx = rope(k_idx[:, :, None, :], pos)[:, :, 0, :]

        y_a = conformer_conv_branch(u_conv, conv_ln_g[l], conv_ln_b[l], conv_dw_w[l],
                                    conv_dw_b[l], conv_pw_out[l])
        y_b = dsa_sparse_attention(q, k, v, q_idx, k_idx, w_idx, attn_w_o[l])
        g_a, g_b = jnp.split(jax.nn.sigmoid(gates), 2, axis=-1)
        stream = stream + (g_a * y_a + g_b * y_b) @ w_merge_out[l]

        h = rms_norm(stream, ffn_norm_g[l])
        up = causal_depthwise_conv(h @ ffn_up[l], ffn_dw_w[l], ffn_dw_b[l])
        a, b_ = jnp.split(up, 2, axis=-1)
        stream = stream + (jax.nn.silu(a) * b_) @ ffn_down[l]

    out = rms_norm(stream, final_norm_g)
    return out[:, N_META:]
```

```python
import functools

import jax
import jax.numpy as jnp
from jax import lax
from jax.experimental import pallas as pl
from jax.experimental.pallas import tpu as pltpu

N_META = 16
CONV_WIDTH = 31
N_HEADS = 8
HEAD_DIM = 128
N_KV_HEADS = 2
N_IDX_HEADS = 8
IDX_DIM = 64
TOPK_MAX = 256
FFN_CONV_WIDTH = 3
ROPE_THETA = 10000.0
EPS = 1e-6

LANES = 128
SUBLANES = 8
Q_TILE = 128
KEY_CHUNK = 512
ROW_TILE = 384
CONV_HALO = 32
CONV_ROWS = 64
FFN_CHUNK = 256
VMEM_LIMIT = 56 * 1024 * 1024

F32 = jnp.float32
BF16 = jnp.bfloat16
NEG_INF = float("-inf")


def _rope_tables(n_pos, dim, reps):
    half = dim // 2
    inv = ROPE_THETA ** (-jnp.arange(half, dtype=F32) / half)
    ang = jnp.arange(n_pos, dtype=jnp.int32).astype(F32)[:, None] * inv[None, :]
    cos, sin = jnp.cos(ang), jnp.sin(ang)
    cos_f = jnp.concatenate([cos, cos], axis=-1)
    sin_f = jnp.concatenate([-sin, sin], axis=-1)
    return jnp.tile(cos_f, (1, reps)), jnp.tile(sin_f, (1, reps))


def _rope128(x, cos, sin):
    return x * cos + pltpu.roll(x, HEAD_DIM // 2, axis=1) * sin


def _rope64(x, cos, sin, first_half):
    partner = jnp.where(first_half, pltpu.roll(x, LANES - IDX_DIM // 2, axis=1),
                        pltpu.roll(x, IDX_DIM // 2, axis=1))
    return x * cos + partner * sin


def _in_proj_kernel(x_ref, g_ref, wu_ref, wq_ref, wkv_ref, wqi_ref, wkw_ref, wg_ref,
                    c128_ref, s128_ref, c64_ref, s64_ref,
                    glu_ref, q_ref, k_ref, v_ref, qi_ref, kw_ref, gate_ref):
    d_conv = glu_ref.shape[-1]
    kv_w = k_ref.shape[-1]
    x = x_ref[...]
    h = x * lax.rsqrt(jnp.mean(x * x, axis=-1, keepdims=True) + EPS) * g_ref[...]
    hb = h.astype(BF16)

    u = jnp.dot(hb, wu_ref[...], preferred_element_type=F32)
    glu_ref[...] = u[:, :d_conv] * jax.nn.sigmoid(u[:, d_conv:])

    gate_ref[...] = jax.nn.sigmoid(jnp.dot(hb, wg_ref[...], preferred_element_type=F32))

    c128, s128 = c128_ref[...], s128_ref[...]
    q = jnp.dot(hb, wq_ref[...], preferred_element_type=F32)
    for hd in range(q.shape[-1] // HEAD_DIM):
        sl = slice(hd * HEAD_DIM, (hd + 1) * HEAD_DIM)
        q_ref[:, sl] = _rope128(q[:, sl], c128, s128).astype(q_ref.dtype)

    kv = jnp.dot(hb, wkv_ref[...], preferred_element_type=F32)
    for hd in range(kv_w // HEAD_DIM):
        sl = slice(hd * HEAD_DIM, (hd + 1) * HEAD_DIM)
        k_ref[:, sl] = _rope128(kv[:, sl], c128, s128).astype(k_ref.dtype)
    v_ref[...] = kv[:, kv_w:].astype(v_ref.dtype)

    c64, s64 = c64_ref[...], s64_ref[...]
    lane = lax.broadcasted_iota(jnp.int32, (x.shape[0], LANES), 1)
    first_half = (lane & (IDX_DIM - 1)) < (IDX_DIM // 2)
    qi = jnp.dot(h, wqi_ref[...], preferred_element_type=F32, precision=lax.Precision.HIGHEST)
    for blk in range(qi.shape[-1] // LANES):
        sl = slice(blk * LANES, (blk + 1) * LANES)
        qi_ref[:, sl] = _rope64(qi[:, sl], c64, s64, first_half)
    kw = jnp.dot(h, wkw_ref[...], preferred_element_type=F32, precision=lax.Precision.HIGHEST)
    kw_ref[...] = jnp.where(lane < IDX_DIM, _rope64(kw, c64, s64, first_half), kw)


def _in_proj(stream, g, wu, wq, wkv, wqi, wkw, wg, tabs, tm):
    bsz, lp, d = stream.shape
    nt = lp // tm
    row = lambda w: pl.BlockSpec((None, tm, w), lambda b, i: (b, i, 0))
    full = lambda a: pl.BlockSpec(a.shape, lambda b, i: (0,) * a.ndim)
    tab = pl.BlockSpec((tm, LANES), lambda b, i: (i, 0))
    outs = [(wu.shape[1] // 2, F32), (wq.shape[1], BF16), (wkv.shape[1] // 2, BF16),
            (wkv.shape[1] // 2, BF16), (wqi.shape[1], F32), (LANES, F32), (wg.shape[1], F32)]
    return pl.pallas_call(
        _in_proj_kernel,
        grid=(bsz, nt),
        in_specs=[row(d), full(g), full(wu), full(wq), full(wkv), full(wqi), full(wkw), full(wg),
                  tab, tab, tab, tab],
        out_specs=[row(w) for w, _ in outs],
        out_shape=[jax.ShapeDtypeStruct((bsz, lp, w), dt) for w, dt in outs],
        compiler_params=pltpu.CompilerParams(
            dimension_semantics=("parallel", "parallel"), vmem_limit_bytes=VMEM_LIMIT),
        name="in_proj",
    )(stream, g, wu, wq, wkv, wqi, wkw, wg, *tabs)


def _conv_kernel(glu_ref, dww_ref, dwb_ref, lng_ref, lnb_ref, pw_ref, ya_ref, buf_ref, conv_ref):
    tt, c = glu_ref.shape
    i = pl.program_id(1)

    @pl.when(i == 0)
    def _():
        buf_ref[0:CONV_HALO, :] = jnp.zeros((CONV_HALO, c), F32)

    @pl.when(i > 0)
    def _():
        buf_ref[0:CONV_HALO, :] = buf_ref[tt:tt + CONV_HALO, :]

    buf_ref[CONV_HALO:, :] = glu_ref[...]

    base = CONV_HALO - (CONV_WIDTH - 1)
    for cb in range(c // LANES):
        cs = slice(cb * LANES, (cb + 1) * LANES)
        w_c = dww_ref[:, cs]
        b_c = dwb_ref[:, cs]
        for rb in range(tt // CONV_ROWS):
            r0 = rb * CONV_ROWS
            acc = jnp.broadcast_to(b_c, (CONV_ROWS, LANES))
            for j in range(CONV_WIDTH):
                acc = acc + buf_ref[r0 + base + j:r0 + base + j + CONV_ROWS, cs] * w_c[j:j + 1, :]
            conv_ref[r0:r0 + CONV_ROWS, cs] = acc

    hc = conv_ref[...]
    mu = jnp.mean(hc, axis=-1, keepdims=True)
    dlt = hc - mu
    var = jnp.mean(dlt * dlt, axis=-1, keepdims=True)
    hn = dlt * lax.rsqrt(var + EPS) * lng_ref[...] + lnb_ref[...]
    hs = hn * jax.nn.sigmoid(hn)
    ya_ref[...] = jnp.dot(hs.astype(BF16), pw_ref[...], preferred_element_type=F32)


def _conv_branch(glu, dww, dwb, lng, lnb, pw, tt):
    bsz, lp, c = glu.shape
    full = lambda a: pl.BlockSpec(a.shape, lambda b, i: (0,) * a.ndim)
    return pl.pallas_call(
        _conv_kernel,
        grid=(bsz, lp // tt),
        in_specs=[pl.BlockSpec((None, tt, c), lambda b, i: (b, i, 0)),
                  full(dww), full(dwb), full(lng), full(lnb), full(pw)],
        out_specs=pl.BlockSpec((None, tt, pw.shape[1]), lambda b, i: (b, i, 0)),
        out_shape=jax.ShapeDtypeStruct((bsz, lp, pw.shape[1]), F32),
        scratch_shapes=[pltpu.VMEM((CONV_HALO + tt, c), F32), pltpu.VMEM((tt, c), F32)],
        compiler_params=pltpu.CompilerParams(
            dimension_semantics=("parallel", "arbitrary"), vmem_limit_bytes=VMEM_LIMIT),
        name="conv_branch",
    )(glu, dww, dwb, lng, lnb, pw)


def _row_count(s_ref, thr, n_chunks, strict):
    tq = s_ref.shape[0]
    thr = jnp.broadcast_to(thr, (tq, LANES))

    def body(c, acc):
        off = pl.multiple_of(c * KEY_CHUNK, KEY_CHUNK)
        for l in range(KEY_CHUNK // LANES):
            blk = s_ref[:, pl.ds(off + l * LANES, LANES)]
            hit = (blk > thr) if strict else (blk >= thr)
            acc = acc + jnp.where(hit, 1.0, 0.0)
        return acc

    acc = lax.fori_loop(0, n_chunks, body, jnp.zeros((tq, LANES), F32))
    return jnp.sum(acc, axis=1, keepdims=True)


def _row_min_above(s_ref, lo, n_chunks):
    tq = s_ref.shape[0]
    lo = jnp.broadcast_to(lo, (tq, LANES))

    def body(c, acc):
        off = pl.multiple_of(c * KEY_CHUNK, KEY_CHUNK)
        for l in range(KEY_CHUNK // LANES):
            blk = s_ref[:, pl.ds(off + l * LANES, LANES)]
            acc = jnp.minimum(acc, jnp.where(blk > lo, blk, jnp.inf))
        return acc

    acc = lax.fori_loop(0, n_chunks, body, jnp.full((tq, LANES), jnp.inf, F32))
    return jnp.min(acc, axis=1, keepdims=True)


def _dsa_kernel(q_ref, qi_ref, kwq_ref, k_ref, v_ref, kwk_ref, o_ref,
                s_ref, bias_ref, m_ref, l_ref, acc_ref, *, seq_len, top_k, bisect_steps):
    tq = q_ref.shape[0]
    j = pl.program_id(1)
    q0 = j * tq
    n_chunks = (q0 + tq + KEY_CHUNK - 1) // KEY_CHUNK
    rep = N_HEADS // N_KV_HEADS
    need = float(top_k - N_META)

    q_pos = q0 + lax.broadcasted_iota(jnp.int32, (tq, 1), 0)
    w_idx = kwq_ref[:, IDX_DIM:IDX_DIM + N_IDX_HEADS] * (N_IDX_HEADS ** -0.5 * IDX_DIM ** -0.5)
    qi = qi_ref[...]

    def score_body(c, carry):
        off = pl.multiple_of(c * KEY_CHUNK, KEY_CHUNK)
        kidx = kwk_ref[pl.ds(off, KEY_CHUNK), 0:IDX_DIM]
        sc = jnp.zeros((tq, KEY_CHUNK), F32)
        for hd in range(N_IDX_HEADS):
            dots = lax.dot_general(qi[:, hd * IDX_DIM:(hd + 1) * IDX_DIM], kidx,
                                   (((1,), (1,)), ((), ())), preferred_element_type=F32,
                                   precision=lax.Precision.HIGHEST)
            sc = sc + w_idx[:, hd:hd + 1] * jnp.maximum(dots, 0.0)
        key_pos = off + lax.broadcasted_iota(jnp.int32, (tq, KEY_CHUNK), 1)
        cand = (key_pos <= q_pos) & (key_pos < seq_len) & (key_pos >= N_META)
        s_ref[:, pl.ds(off, KEY_CHUNK)] = jnp.where(cand, sc, NEG_INF)
        return carry

    lax.fori_loop(0, n_chunks, score_body, 0)

    select_all = (q_pos + 1) <= top_k
    thr_ref, eq_take_ref = m_ref, l_ref

    thr_ref[:, 0:1] = jnp.full((tq, 1), NEG_INF, F32)
    eq_take_ref[:, 0:1] = jnp.full((tq, 1), jnp.inf, F32)

    @pl.when(q0 + tq > top_k)
    def _():
        def rng_body(c, carry):
            mn, mx = carry
            off = pl.multiple_of(c * KEY_CHUNK, KEY_CHUNK)
            for l in range(KEY_CHUNK // LANES):
                blk = s_ref[:, pl.ds(off + l * LANES, LANES)]
                mx = jnp.maximum(mx, blk)
                mn = jnp.minimum(mn, jnp.where(blk > NEG_INF, blk, jnp.inf))
            return mn, mx

        mn, mx = lax.fori_loop(0, n_chunks, rng_body,
                               (jnp.full((tq, LANES), jnp.inf, F32), jnp.full((tq, LANES), NEG_INF, F32)))
        row_min = jnp.min(mn, axis=1, keepdims=True)
        row_max = jnp.max(mx, axis=1, keepdims=True)
        lo0 = jnp.where(select_all, 0.0, jnp.minimum(row_min, 0.0) * 2.0 - 1.0)
        hi0 = jnp.where(select_all, 0.0, row_max)

        def bisect(_, carry):
            lo, hi = carry
            mid = 0.5 * lo + 0.5 * hi
            ge = _row_count(s_ref, mid, n_chunks, True) >= need
            return jnp.where(ge, mid, lo), jnp.where(ge, hi, mid)

        def snap(lo):
            cand = _row_min_above(s_ref, lo, n_chunks)
            done = select_all | (_row_count(s_ref, cand, n_chunks, True) < need)
            return cand, done

        def cond(state):
            return state[3] > 0.0

        def step(state):
            lo, hi, _, _ = state
            lo, hi = lax.fori_loop(0, bisect_steps, bisect, (lo, hi))
            cand, done = snap(lo)
            return jnp.where(done, lo, cand), hi, cand, jnp.sum(jnp.where(done, 0.0, 1.0))

        _, _, kth, _ = lax.while_loop(cond, step, (lo0, hi0, hi0, jnp.float32(1.0)))

        n_gt = _row_count(s_ref, kth, n_chunks, True)
        n_ge = _row_count(s_ref, kth, n_chunks, False)
        thr_ref[:, 0:1] = jnp.where(select_all, NEG_INF, kth)
        eq_take_ref[:, 0:1] = jnp.where(select_all | (n_ge <= need), jnp.inf, need - n_gt)

    thr = thr_ref[:, 0:1]
    eq_take = eq_take_ref[:, 0:1]
    any_tie = jnp.max(jnp.where(eq_take < jnp.inf, 1.0, 0.0)) > 0.0

    def bias_body(c, carry):
        off = pl.multiple_of(c * KEY_CHUNK, KEY_CHUNK)
        sc = s_ref[:, pl.ds(off, KEY_CHUNK)]
        key_pos = off + lax.broadcasted_iota(jnp.int32, (tq, KEY_CHUNK), 1)
        meta = (key_pos <= q_pos) & (key_pos < N_META) & (key_pos < seq_len)
        bias_ref[:, pl.ds(off, KEY_CHUNK)] = jnp.where(meta | (sc > thr), 0.0, NEG_INF)
        return carry

    lax.fori_loop(0, n_chunks, bias_body, 0)

    @pl.when(jnp.logical_not(any_tie))
    def _():
        def tie_free(c, carry):
            off = pl.multiple_of(c * KEY_CHUNK, KEY_CHUNK)
            sc = s_ref[:, pl.ds(off, KEY_CHUNK)]
            b = bias_ref[:, pl.ds(off, KEY_CHUNK)]
            bias_ref[:, pl.ds(off, KEY_CHUNK)] = jnp.where((sc == thr) & (sc > NEG_INF), 0.0, b)
            return carry

        lax.fori_loop(0, n_chunks, tie_free, 0)

    @pl.when(any_tie)
    def _():
        tri = (lax.broadcasted_iota(jnp.int32, (KEY_CHUNK, KEY_CHUNK), 0)
               <= lax.broadcasted_iota(jnp.int32, (KEY_CHUNK, KEY_CHUNK), 1)).astype(BF16)

        def tie_body(c, seen):
            off = pl.multiple_of(c * KEY_CHUNK, KEY_CHUNK)
            sc = s_ref[:, pl.ds(off, KEY_CHUNK)]
            eq = (sc == thr) & (sc > NEG_INF)
            rank = seen + jnp.dot(eq.astype(BF16), tri, preferred_element_type=F32)
            b = bias_ref[:, pl.ds(off, KEY_CHUNK)]
            bias_ref[:, pl.ds(off, KEY_CHUNK)] = jnp.where(eq & (rank <= eq_take), 0.0, b)
            return seen + jnp.sum(jnp.where(eq, 1.0, 0.0), axis=1, keepdims=True)

        lax.fori_loop(0, n_chunks, tie_body, jnp.zeros((tq, 1), F32))

    m_ref[...] = jnp.full(m_ref.shape, NEG_INF, F32)
    l_ref[...] = jnp.zeros(l_ref.shape, F32)
    acc_ref[...] = jnp.zeros(acc_ref.shape, F32)

    def attn_body(c, carry):
        off = pl.multiple_of(c * KEY_CHUNK, KEY_CHUNK)
        bias = bias_ref[:, pl.ds(off, KEY_CHUNK)]
        for g in range(N_KV_HEADS):
            gs = slice(g * HEAD_DIM, (g + 1) * HEAD_DIM)
            k_g = k_ref[pl.ds(off, KEY_CHUNK), gs]
            v_g = v_ref[pl.ds(off, KEY_CHUNK), gs]
            for r in range(rep):
                hd = g * rep + r
                hs = slice(hd * HEAD_DIM, (hd + 1) * HEAD_DIM)
                logits = lax.dot_general(q_ref[:, hs], k_g, (((1,), (1,)), ((), ())),
                                         preferred_element_type=F32) + bias
                m_old = m_ref[:, hd:hd + 1]
                m_new = jnp.maximum(m_old, jnp.max(logits, axis=1, keepdims=True))
                p = jnp.exp(logits - m_new)
                alpha = jnp.exp(m_old - m_new)
                l_ref[:, hd:hd + 1] = alpha * l_ref[:, hd:hd + 1] + jnp.sum(p, axis=1, keepdims=True)
                acc_ref[:, hs] = alpha * acc_ref[:, hs] + jnp.dot(
                    p.astype(BF16), v_g, preferred_element_type=F32)
                m_ref[:, hd:hd + 1] = m_new
        return carry

    lax.fori_loop(0, n_chunks, attn_body, 0)

    for hd in range(N_HEADS):
        hs = slice(hd * HEAD_DIM, (hd + 1) * HEAD_DIM)
        o_ref[:, hs] = (acc_ref[:, hs] / l_ref[:, hd:hd + 1]).astype(o_ref.dtype)


def _dsa(q, qi, kw, k, v, seq_len, top_k):
    bsz, lp, dq = q.shape
    lk = -(-lp // KEY_CHUNK) * KEY_CHUNK
    pad = lambda a: jnp.pad(a, ((0, 0), (0, lk - lp), (0, 0)))
    k_p, v_p, kw_p = pad(k), pad(v), pad(kw)
    qrow = lambda w: pl.BlockSpec((None, Q_TILE, w), lambda b, j: (b, j, 0))
    keys = lambda w: pl.BlockSpec((None, lk, w), lambda b, j: (b, 0, 0))
    kern = functools.partial(_dsa_kernel, seq_len=seq_len, top_k=top_k, bisect_steps=6)
    return pl.pallas_call(
        kern,
        grid=(bsz, lp // Q_TILE),
        in_specs=[qrow(dq), qrow(qi.shape[-1]), qrow(LANES), keys(k.shape[-1]), keys(v.shape[-1]),
                  keys(LANES)],
        out_specs=qrow(dq),
        out_shape=jax.ShapeDtypeStruct((bsz, lp, dq), BF16),
        scratch_shapes=[pltpu.VMEM((Q_TILE, lk), F32), pltpu.VMEM((Q_TILE, lk), F32),
                        pltpu.VMEM((Q_TILE, N_HEADS), F32), pltpu.VMEM((Q_TILE, N_HEADS), F32),
                        pltpu.VMEM((Q_TILE, dq), F32)],
        compiler_params=pltpu.CompilerParams(
            dimension_semantics=("parallel", "arbitrary"), vmem_limit_bytes=VMEM_LIMIT),
        name="dsa_attention",
    )(q, qi, kw, k_p, v_p, kw_p)


def _merge_ffn_kernel(s_ref, ya_ref, ao_ref, gate_ref, wo_ref, wm_ref, fng_ref, up_ref, dww_ref,
                      dwb_ref, down_ref, fin_ref, out_ref, carry_ref, ubuf_ref, hb_ref, acc_ref):
    tm, d = s_ref.shape
    d_ff = down_ref.shape[0]
    i = pl.program_id(1)

    @pl.when(i == 0)
    def _():
        carry_ref[...] = jnp.zeros(carry_ref.shape, F32)

    y_b = jnp.dot(ao_ref[...], wo_ref[...], preferred_element_type=F32)
    mix = gate_ref[:, :d] * ya_ref[...] + gate_ref[:, d:] * y_b
    stream = s_ref[...] + jnp.dot(mix.astype(BF16), wm_ref[...], preferred_element_type=F32)
    h = stream * lax.rsqrt(jnp.mean(stream * stream, axis=-1, keepdims=True) + EPS) * fng_ref[...]
    hb_ref[...] = h.astype(BF16)
    acc_ref[...] = stream

    def conv3(u, col):
        ubuf_ref[0:SUBLANES, :] = carry_ref[:, pl.ds(col, FFN_CHUNK)]
        ubuf_ref[SUBLANES:, :] = u
        carry_ref[:, pl.ds(col, FFN_CHUNK)] = u[tm - SUBLANES:, :]
        w = dww_ref[:, pl.ds(col, FFN_CHUNK)]
        out = dwb_ref[:, pl.ds(col, FFN_CHUNK)] + w[2:3, :] * u
        out = out + w[1:2, :] * ubuf_ref[SUBLANES - 1:SUBLANES - 1 + tm, :]
        return out + w[0:1, :] * ubuf_ref[SUBLANES - 2:SUBLANES - 2 + tm, :]

    def chunk(c, carry):
        col_a = pl.multiple_of(c * FFN_CHUNK, FFN_CHUNK)
        col_b = pl.multiple_of(d_ff + c * FFN_CHUNK, FFN_CHUNK)
        hb = hb_ref[...]
        a = conv3(jnp.dot(hb, up_ref[:, pl.ds(col_a, FFN_CHUNK)], preferred_element_type=F32), col_a)
        b = conv3(jnp.dot(hb, up_ref[:, pl.ds(col_b, FFN_CHUNK)], preferred_element_type=F32), col_b)
        act = (a * jax.nn.sigmoid(a) * b).astype(BF16)
        acc_ref[...] += jnp.dot(act, down_ref[pl.ds(col_a, FFN_CHUNK), :], preferred_element_type=F32)
        return carry

    lax.fori_loop(0, d_ff // FFN_CHUNK, chunk, 0)

    y = acc_ref[...]
    out_ref[...] = y * lax.rsqrt(jnp.mean(y * y, axis=-1, keepdims=True) + EPS) * fin_ref[...]


def _merge_ffn(stream, ya, ao, gates, wo, wm, fng, up, dww, dwb, down, fin, tm):
    bsz, lp, d = stream.shape
    row = lambda w: pl.BlockSpec((None, tm, w), lambda b, i: (b, i, 0))
    full = lambda a: pl.BlockSpec(a.shape, lambda b, i: (0,) * a.ndim)
    return pl.pallas_call(
        _merge_ffn_kernel,
        grid=(bsz, lp // tm),
        in_specs=[row(d), row(d), row(d), row(2 * d), full(wo), full(wm), full(fng), full(up),
                  full(dww), full(dwb), full(down), full(fin)],
        out_specs=row(d),
        out_shape=jax.ShapeDtypeStruct((bsz, lp, d), F32),
        scratch_shapes=[pltpu.VMEM((SUBLANES, up.shape[1]), F32),
                        pltpu.VMEM((SUBLANES + tm, FFN_CHUNK), F32),
                        pltpu.VMEM((tm, d), BF16), pltpu.VMEM((tm, d), F32)],
        compiler_params=pltpu.CompilerParams(
            dimension_semantics=("parallel", "arbitrary"), vmem_limit_bytes=VMEM_LIMIT),
        name="merge_ffn",
    )(stream, ya, ao, gates, wo, wm, fng, up, dww, dwb, down, fin)


def _row_tile(lp):
    for t in (ROW_TILE, 256, LANES):
        if lp % t == 0:
            return t
    raise ValueError(f"padded length {lp} is not a multiple of {LANES}")


def kernel(x, meta_tokens, mix_norm_g, w_in, conv_ln_g, conv_ln_b, conv_dw_w, conv_dw_b, conv_pw_out,
           attn_w_o, w_merge_out, ffn_norm_g, ffn_up, ffn_dw_w, ffn_dw_b, ffn_down, final_norm_g):
    bsz, seq, d = x.shape
    depth = w_in.shape[0]
    seq_len = seq + N_META
    lp = -(-seq_len // LANES) * LANES
    tm = _row_tile(lp)
    top_k = min(TOPK_MAX, seq_len // 4)
    assert top_k > N_META and conv_dw_w.shape[1] == CONV_WIDTH and ffn_dw_w.shape[1] == FFN_CONV_WIDTH
    d_conv = conv_pw_out.shape[1]
    d_ff = ffn_down.shape[1]
    assert d_ff % FFN_CHUNK == 0 and d % LANES == 0

    meta = jnp.broadcast_to(meta_tokens.astype(x.dtype)[None], (bsz, N_META, d))
    stream = jnp.concatenate([meta, x, jnp.zeros((bsz, lp - seq_len, d), x.dtype)], axis=1)
    tabs = _rope_tables(lp, HEAD_DIM, 1) + _rope_tables(lp, IDX_DIM, LANES // IDX_DIM)

    n_q, n_kv = N_HEADS * HEAD_DIM, N_KV_HEADS * HEAD_DIM
    n_qi = N_IDX_HEADS * IDX_DIM
    edges = [0, 2 * d_conv]
    for wdt in (n_q, n_kv, n_kv, n_qi, IDX_DIM, N_IDX_HEADS, 2 * d):
        edges.append(edges[-1] + wdt)
    row2 = lambda a: a.reshape(1, -1).astype(F32)

    for l in range(depth):
        w = w_in[l]
        wu = w[:, edges[0]:edges[1]].astype(BF16)
        wq = (w[:, edges[1]:edges[2]] * (HEAD_DIM ** -0.5)).astype(BF16)
        wkv = w[:, edges[2]:edges[4]].astype(BF16)
        wqi = w[:, edges[4]:edges[5]]
        wkw = jnp.pad(w[:, edges[5]:edges[7]], ((0, 0), (0, LANES - IDX_DIM - N_IDX_HEADS)))
        wg = w[:, edges[7]:edges[8]].astype(BF16)

        glu, q, k, v, qi, kw, gates = _in_proj(stream, row2(mix_norm_g[l]), wu, wq, wkv, wqi, wkw, wg,
                                               tabs, tm)
        ya = _conv_branch(glu, conv_dw_w[l], row2(conv_dw_b[l]), row2(conv_ln_g[l]), row2(conv_ln_b[l]),
                          conv_pw_out[l].astype(BF16), tm)
        ao = _dsa(q, qi, kw, k, v, seq_len, top_k)
        last = l == depth - 1
        stream = _merge_ffn(stream, ya, ao, gates, attn_w_o[l].astype(BF16), w_merge_out[l].astype(BF16),
                            row2(ffn_norm_g[l]), ffn_up[l].astype(BF16), ffn_dw_w[l], row2(ffn_dw_b[l]),
                            ffn_down[l].astype(BF16), row2(final_norm_g), tm)
        assert last, "the fused final norm assumes a single layer"
    return stream[:, N_META:seq_len]
```

```python
import functools
import math

import jax
import jax.numpy as jnp
from jax import lax
from jax.experimental import pallas as pl
from jax.experimental.pallas import tpu as pltpu

N_META = 16
CONV_WIDTH = 31
N_HEADS = 8
HEAD_DIM = 128
N_KV_HEADS = 2
N_IDX_HEADS = 8
IDX_DIM = 64
TOPK_MAX = 256
FFN_CONV_WIDTH = 3
ROPE_THETA = 10000.0
EPS = 1e-6

LANES = 128
SUBLANES = 8
Q_TILE = 128
KEY_CHUNK = 512
ROW_TILE = 384
CONV_HALO = 32
CONV_ROWS = 64
FFN_CHUNK = 256
SPLIT_W = 4 * IDX_DIM
VMEM_LIMIT = 56 * 1024 * 1024

F32 = jnp.float32
BF16 = jnp.bfloat16
NEG_INF = float("-inf")
MASK_BIAS = -1e30
F32_LOWEST = float(jnp.finfo(jnp.float32).min)
NT_DIMS = (((1,), (1,)), ((), ()))


def _rope_tables(n_pos, dim, reps):
    half = dim // 2
    inv = ROPE_THETA ** (-jnp.arange(half, dtype=F32) / half)
    ang = jnp.arange(n_pos, dtype=jnp.int32).astype(F32)[:, None] * inv[None, :]
    cos, sin = jnp.cos(ang), jnp.sin(ang)
    cos_f = jnp.concatenate([cos, cos], axis=-1)
    sin_f = jnp.concatenate([-sin, sin], axis=-1)
    return jnp.tile(cos_f, (1, reps)), jnp.tile(sin_f, (1, reps))


def _rope128(x, cos, sin):
    return x * cos + pltpu.roll(x, HEAD_DIM // 2, axis=1) * sin


def _rope64(x, cos, sin, first_half):
    partner = jnp.where(first_half, pltpu.roll(x, LANES - IDX_DIM // 2, axis=1),
                        pltpu.roll(x, IDX_DIM // 2, axis=1))
    return x * cos + partner * sin


def _hi_lo(x):
    hi = x.astype(BF16).astype(F32)
    return hi, x - hi


def _in_proj_kernel(x_ref, g_ref, wu_ref, wq_ref, wkv_ref, wqih_ref, wqil_ref, wkwh_ref, wkwl_ref, wg_ref,
                    c128_ref, s128_ref, c64_ref, s64_ref,
                    glu_ref, q_ref, k_ref, v_ref, qs_ref, kk_ref, w_ref, gate_ref):
    d_conv = glu_ref.shape[-1]
    kv_w = k_ref.shape[-1]
    x = x_ref[...]
    h = x * lax.rsqrt(jnp.mean(x * x, axis=-1, keepdims=True) + EPS) * g_ref[...]
    hb = h.astype(BF16)
    hl = (h - hb.astype(F32)).astype(BF16)

    u = jnp.dot(hb, wu_ref[...], preferred_element_type=F32)
    glu_ref[...] = (u[:, :d_conv] * jax.nn.sigmoid(u[:, d_conv:])).astype(glu_ref.dtype)

    gate_ref[...] = jax.nn.sigmoid(
        jnp.dot(hb, wg_ref[...], preferred_element_type=F32)).astype(gate_ref.dtype)

    c128, s128 = c128_ref[...], s128_ref[...]
    q = jnp.dot(hb, wq_ref[...], preferred_element_type=F32)
    for hd in range(q.shape[-1] // HEAD_DIM):
        sl = slice(hd * HEAD_DIM, (hd + 1) * HEAD_DIM)
        q_ref[:, sl] = _rope128(q[:, sl], c128, s128).astype(q_ref.dtype)

    kv = jnp.dot(hb, wkv_ref[...], preferred_element_type=F32)
    for hd in range(kv_w // HEAD_DIM):
        sl = slice(hd * HEAD_DIM, (hd + 1) * HEAD_DIM)
        k_ref[:, sl] = _rope128(kv[:, sl], c128, s128).astype(k_ref.dtype)
    v_ref[...] = kv[:, kv_w:].astype(v_ref.dtype)

    def split_dot(w_hi_ref, w_lo_ref):
        w_hi = w_hi_ref[...]
        return (jnp.dot(hb, w_hi, preferred_element_type=F32) + jnp.dot(hl, w_hi, preferred_element_type=F32)
                + jnp.dot(hb, w_lo_ref[...], preferred_element_type=F32))

    c64, s64 = c64_ref[...], s64_ref[...]
    lane = lax.broadcasted_iota(jnp.int32, (x.shape[0], LANES), 1)
    first_half = (lane & (IDX_DIM - 1)) < (IDX_DIM // 2)
    low_head = lane < IDX_DIM

    qi = split_dot(wqih_ref, wqil_ref)
    for blk in range(qi.shape[-1] // LANES):
        xr = _rope64(qi[:, blk * LANES:(blk + 1) * LANES], c64, s64, first_half)
        xs = pltpu.roll(xr, IDX_DIM, axis=1)
        for sub, dup in enumerate((jnp.where(low_head, xr, xs), jnp.where(low_head, xs, xr))):
            hi, lo = _hi_lo(dup)
            c0 = (2 * blk + sub) * SPLIT_W
            qs_ref[:, c0:c0 + LANES] = hi.astype(BF16)
            qs_ref[:, c0 + LANES:c0 + 2 * LANES] = lo.astype(BF16)

    kw = split_dot(wkwh_ref, wkwl_ref)
    kr = _rope64(kw, c64, s64, first_half)
    hi, lo = _hi_lo(jnp.where(low_head, kr, pltpu.roll(kr, IDX_DIM, axis=1)))
    mix = jnp.where(low_head, hi, lo).astype(BF16)
    kk_ref[:, 0:LANES] = mix
    kk_ref[:, LANES:2 * LANES] = mix
    w_ref[...] = kw[:, IDX_DIM:IDX_DIM + N_IDX_HEADS] * (N_IDX_HEADS ** -0.5 * IDX_DIM ** -0.5)


def _in_proj(stream, g, wu, wq, wkv, wqih, wqil, wkwh, wkwl, wg, tabs, tm):
    bsz, lp, d = stream.shape
    nt = lp // tm
    row = lambda w: pl.BlockSpec((None, tm, w), lambda b, i: (b, i, 0))
    full = lambda a: pl.BlockSpec(a.shape, lambda b, i: (0,) * a.ndim)
    tab = pl.BlockSpec((tm, LANES), lambda b, i: (i, 0))
    outs = [(wu.shape[1] // 2, BF16), (wq.shape[1], BF16), (wkv.shape[1] // 2, BF16),
            (wkv.shape[1] // 2, BF16), (N_IDX_HEADS * SPLIT_W, BF16), (SPLIT_W, BF16),
            (N_IDX_HEADS, F32), (wg.shape[1], BF16)]
    return pl.pallas_call(
        _in_proj_kernel,
        grid=(bsz, nt),
        in_specs=[row(d), full(g), full(wu), full(wq), full(wkv), full(wqih), full(wqil), full(wkwh),
                  full(wkwl), full(wg), tab, tab, tab, tab],
        out_specs=[row(w) for w, _ in outs],
        out_shape=[jax.ShapeDtypeStruct((bsz, lp, w), dt) for w, dt in outs],
        compiler_params=pltpu.CompilerParams(
            dimension_semantics=("parallel", "parallel"), vmem_limit_bytes=VMEM_LIMIT),
        name="in_proj",
    )(stream, g, wu, wq, wkv, wqih, wqil, wkwh, wkwl, wg, *tabs)


def _conv_kernel(glu_ref, dww_ref, dwb_ref, lng_ref, lnb_ref, pw_ref, ya_ref, buf_ref, conv_ref):
    tt, c = glu_ref.shape
    i = pl.program_id(1)

    @pl.when(i == 0)
    def _():
        buf_ref[0:CONV_HALO, :] = jnp.zeros((CONV_HALO, c), F32)

    @pl.when(i > 0)
    def _():
        buf_ref[0:CONV_HALO, :] = buf_ref[tt:tt + CONV_HALO, :]

    buf_ref[CONV_HALO:, :] = glu_ref[...].astype(F32)

    base = CONV_HALO - (CONV_WIDTH - 1)
    for cb in range(c // LANES):
        cs = slice(cb * LANES, (cb + 1) * LANES)
        w_c = dww_ref[:, cs]
        b_c = dwb_ref[:, cs]
        for rb in range(tt // CONV_ROWS):
            r0 = rb * CONV_ROWS
            acc = jnp.broadcast_to(b_c, (CONV_ROWS, LANES))
            for j in range(CONV_WIDTH):
                acc = acc + buf_ref[r0 + base + j:r0 + base + j + CONV_ROWS, cs] * w_c[j:j + 1, :]
            conv_ref[r0:r0 + CONV_ROWS, cs] = acc

    hc = conv_ref[...]
    mu = jnp.mean(hc, axis=-1, keepdims=True)
    dlt = hc - mu
    var = jnp.mean(dlt * dlt, axis=-1, keepdims=True)
    hn = dlt * lax.rsqrt(var + EPS) * lng_ref[...] + lnb_ref[...]
    hs = hn * jax.nn.sigmoid(hn)
    ya_ref[...] = jnp.dot(hs.astype(BF16), pw_ref[...], preferred_element_type=F32).astype(ya_ref.dtype)


def _conv_branch(glu, dww, dwb, lng, lnb, pw, tt):
    bsz, lp, c = glu.shape
    full = lambda a: pl.BlockSpec(a.shape, lambda b, i: (0,) * a.ndim)
    return pl.pallas_call(
        _conv_kernel,
        grid=(bsz, lp // tt),
        in_specs=[pl.BlockSpec((None, tt, c), lambda b, i: (b, i, 0)),
                  full(dww), full(dwb), full(lng), full(lnb), full(pw)],
        out_specs=pl.BlockSpec((None, tt, pw.shape[1]), lambda b, i: (b, i, 0)),
        out_shape=jax.ShapeDtypeStruct((bsz, lp, pw.shape[1]), BF16),
        scratch_shapes=[pltpu.VMEM((CONV_HALO + tt, c), F32), pltpu.VMEM((tt, c), F32)],
        compiler_params=pltpu.CompilerParams(
            dimension_semantics=("parallel", "arbitrary"), vmem_limit_bytes=VMEM_LIMIT),
        name="conv_branch",
    )(glu, dww, dwb, lng, lnb, pw)


def _fold_rows(x, op):
    parts = [x[r * SUBLANES:(r + 1) * SUBLANES, :] for r in range(x.shape[0] // SUBLANES)]
    while len(parts) > 1:
        nxt = [op(parts[i], parts[i + 1]) for i in range(0, len(parts) - 1, 2)]
        parts = nxt + parts[len(parts) - len(parts) % 2:]
    return parts[0]


def _count(st_ref, thr, n_chunks, strict):
    tq = st_ref.shape[1]

    def body(c, acc):
        off = pl.multiple_of(c * KEY_CHUNK, KEY_CHUNK)
        blk = st_ref[pl.ds(off, KEY_CHUNK), :]
        hit = (blk > thr) if strict else (blk >= thr)
        return acc + _fold_rows(jnp.where(hit, 1.0, 0.0), jnp.add)

    acc = lax.fori_loop(0, n_chunks, body, jnp.zeros((SUBLANES, tq), F32))
    return jnp.sum(acc, axis=0, keepdims=True)


def _count_above(st_ref, thr, n_chunks):
    return _count(st_ref, thr, n_chunks, True)


def _count_at_least(st_ref, thr, n_chunks):
    return _count(st_ref, thr, n_chunks, False)


def _min_above(st_ref, lo, n_chunks):
    tq = st_ref.shape[1]

    def body(c, acc):
        off = pl.multiple_of(c * KEY_CHUNK, KEY_CHUNK)
        blk = st_ref[pl.ds(off, KEY_CHUNK), :]
        return jnp.minimum(acc, _fold_rows(jnp.where(blk > lo, blk, jnp.inf), jnp.minimum))

    acc = lax.fori_loop(0, n_chunks, body, jnp.full((SUBLANES, tq), jnp.inf, F32))
    return jnp.min(acc, axis=0, keepdims=True)


def _dsa_kernel(q_ref, qs_ref, wt_ref, kk_ref, k_ref, vt_ref, o_ref,
                st_ref, bias_ref, qsr_ref, qaug_ref, acc_ref, lg_ref, *, seq_len, top_k):
    tq = q_ref.shape[0]
    j = pl.program_id(1)
    q0 = j * tq
    n_chunks = (q0 + tq + KEY_CHUNK - 1) // KEY_CHUNK
    rep = N_HEADS // N_KV_HEADS
    k_sel = float(top_k)
    q_pos = q0 + lax.broadcasted_iota(jnp.int32, (1, tq), 1)

    for hd in range(N_IDX_HEADS):
        qsr_ref[hd * tq:(hd + 1) * tq, :] = qs_ref[:, hd * SPLIT_W:(hd + 1) * SPLIT_W]
    eye = (lax.broadcasted_iota(jnp.int32, (tq, tq), 0)
           == lax.broadcasted_iota(jnp.int32, (tq, tq), 1)).astype(BF16)
    for g in range(N_KV_HEADS):
        for r in range(rep):
            hd = g * rep + r
            qaug_ref[g, r * tq:(r + 1) * tq, 0:HEAD_DIM] = q_ref[:, hd * HEAD_DIM:(hd + 1) * HEAD_DIM]
            qaug_ref[g, r * tq:(r + 1) * tq, HEAD_DIM:] = eye

    w_t = wt_ref[...]

    def score_body(i, carry):
        mn, mx = carry
        for u in range(2):
            c = jnp.minimum(2 * i + u, n_chunks - 1)
            off = pl.multiple_of(c * KEY_CHUNK, KEY_CHUNK)
            dots = lax.dot_general(kk_ref[pl.ds(off, KEY_CHUNK), :], qsr_ref[...], NT_DIMS,
                                   preferred_element_type=F32)
            sc = w_t[0:1, :] * jnp.maximum(dots[:, 0:tq], 0.0)
            for hd in range(1, N_IDX_HEADS):
                sc = sc + w_t[hd:hd + 1, :] * jnp.maximum(dots[:, hd * tq:(hd + 1) * tq], 0.0)
            st_ref[pl.ds(off, KEY_CHUNK), :] = sc
            mn = jnp.minimum(mn, _fold_rows(sc, jnp.minimum))
            mx = jnp.maximum(mx, _fold_rows(sc, jnp.maximum))
        return mn, mx

    mn, mx = lax.fori_loop(0, (n_chunks + 1) // 2, score_body,
                           (jnp.full((SUBLANES, tq), jnp.inf, F32), jnp.full((SUBLANES, tq), NEG_INF, F32)))
    row_min = jnp.min(mn, axis=0, keepdims=True)
    row_max = jnp.max(mx, axis=0, keepdims=True)

    meta_pos = lax.broadcasted_iota(jnp.int32, (N_META, tq), 0)
    st_ref[0:N_META, :] = jnp.where(meta_pos <= q_pos, jnp.inf, NEG_INF)

    def causal_body(blk, carry):
        off = pl.multiple_of(blk * tq, tq)
        key_pos = off + lax.broadcasted_iota(jnp.int32, (tq, tq), 0)
        keep = (key_pos <= q_pos) & (key_pos < seq_len)
        st_ref[pl.ds(off, tq), :] = jnp.where(keep, st_ref[pl.ds(off, tq), :], NEG_INF)
        return carry

    lax.fori_loop(j, n_chunks * (KEY_CHUNK // tq), causal_body, 0)

    select_all = (q_pos + 1) <= top_k

    def search():
        lo0 = jnp.where(select_all, 0.0, jnp.minimum(row_min, 0.0) * 2.0 - 1.0)
        hi0 = jnp.where(select_all, 0.0, row_max)

        def bisect(_, carry):
            lo, hi = carry
            mid = 0.5 * lo + 0.5 * hi
            ge = _count_above(st_ref, mid, n_chunks) >= k_sel
            return jnp.where(ge, mid, lo), jnp.where(ge, hi, mid)

        def snap(lo, hi):
            cand = _min_above(st_ref, lo, n_chunks)
            done = select_all | (_count_above(st_ref, cand, n_chunks) < k_sel)
            return jnp.where(done, lo, cand), hi, cand, jnp.sum(jnp.where(done, 0.0, 1.0))

        def more(state):
            lo, hi = lax.fori_loop(0, 4, bisect, state[:2])
            return snap(lo, hi)

        lo, hi = lax.fori_loop(0, 10, bisect, (lo0, hi0))
        kth = lax.while_loop(lambda s: s[3] > 0.0, more, snap(lo, hi))[2]

        n_gt = _count_above(st_ref, kth, n_chunks)
        n_ge = _count_at_least(st_ref, kth, n_chunks)
        thr = jnp.where(select_all, F32_LOWEST, kth)
        eq_take = jnp.where(select_all | (n_ge <= k_sel), jnp.inf, k_sel - n_gt)
        return thr, eq_take

    def keep_all():
        return jnp.full((1, tq), F32_LOWEST, F32), jnp.full((1, tq), jnp.inf, F32)

    thr, eq_take = lax.cond(q0 + tq > top_k, search, keep_all)
    any_tie = jnp.max(jnp.where(eq_take < jnp.inf, 1.0, 0.0)) > 0.0

    @pl.when(jnp.logical_not(any_tie))
    def _():
        def bias_body(c, carry):
            off = pl.multiple_of(c * KEY_CHUNK, KEY_CHUNK)
            sel = st_ref[pl.ds(off, KEY_CHUNK), :] >= thr
            bias_ref[pl.ds(off, KEY_CHUNK), :] = jnp.where(sel, 0.0, MASK_BIAS).astype(BF16)
            return carry

        lax.fori_loop(0, n_chunks, bias_body, 0)

    @pl.when(any_tie)
    def _():
        tri = (lax.broadcasted_iota(jnp.int32, (KEY_CHUNK, KEY_CHUNK), 1)
               <= lax.broadcasted_iota(jnp.int32, (KEY_CHUNK, KEY_CHUNK), 0)).astype(BF16)

        def tie_body(c, seen):
            off = pl.multiple_of(c * KEY_CHUNK, KEY_CHUNK)
            sc = st_ref[pl.ds(off, KEY_CHUNK), :]
            eq = jnp.where(sc == thr, 1.0, 0.0)
            rank = seen + jnp.dot(tri, eq.astype(BF16), preferred_element_type=F32)
            sel = (sc > thr) | ((sc == thr) & (rank <= eq_take))
            bias_ref[pl.ds(off, KEY_CHUNK), :] = jnp.where(sel, 0.0, MASK_BIAS).astype(BF16)
            return seen + jnp.sum(eq, axis=0, keepdims=True)

        lax.fori_loop(0, n_chunks, tie_body, jnp.zeros((1, tq), F32))

    acc_ref[...] = jnp.zeros(acc_ref.shape, F32)

    def qk(c, slot):
        off = pl.multiple_of(c * KEY_CHUNK, KEY_CHUNK)
        bias = bias_ref[pl.ds(off, KEY_CHUNK), :]
        for g in range(N_KV_HEADS):
            k_g = k_ref[pl.ds(off, KEY_CHUNK), g * HEAD_DIM:(g + 1) * HEAD_DIM]
            lg_ref[slot, g] = lax.dot_general(jnp.concatenate([k_g, bias], axis=1), qaug_ref[g], NT_DIMS,
                                              preferred_element_type=F32)

    def softmax_pv(c, slot, stats):
        off = pl.multiple_of(c * KEY_CHUNK, KEY_CHUNK)
        out = []
        for g in range(N_KV_HEADS):
            m_old, l_old = stats[2 * g], stats[2 * g + 1]
            logits = lg_ref[slot, g]
            m_new = jnp.maximum(m_old, jnp.max(logits, axis=0, keepdims=True))
            p = jnp.exp2(logits - m_new)
            alpha = jnp.exp2(m_old - m_new)
            l_new = alpha * l_old + jnp.sum(p, axis=0, keepdims=True)
            pv = jnp.dot(vt_ref[g, :, pl.ds(off, KEY_CHUNK)], p.astype(BF16), preferred_element_type=F32)
            acc_ref[g] = alpha * acc_ref[g] + pv
            out += [m_new, l_new]
        return tuple(out)

    last = n_chunks - 1
    qk(0, 0)

    def attn_body(i, stats):
        c = 2 * i
        qk(jnp.minimum(c + 1, last), 1)
        stats = softmax_pv(c, 0, stats)

        def second_half(stats):
            qk(jnp.minimum(c + 2, last), 0)
            return softmax_pv(c + 1, 1, stats)

        return lax.cond(c + 1 < n_chunks, second_half, lambda s: s, stats)

    init = (jnp.full((1, rep * tq), NEG_INF, F32), jnp.zeros((1, rep * tq), F32)) * N_KV_HEADS
    stats = lax.fori_loop(0, (n_chunks + 1) // 2, attn_body, init)

    for g in range(N_KV_HEADS):
        o_t = acc_ref[g] / stats[2 * g + 1]
        for r in range(rep):
            hd = g * rep + r
            o_ref[:, hd * HEAD_DIM:(hd + 1) * HEAD_DIM] = o_t[:, r * tq:(r + 1) * tq].T.astype(o_ref.dtype)


def _dsa(q, qs, w_t, kk, k, v_t, seq_len, top_k):
    bsz, lp, dq = q.shape
    lk = kk.shape[1]
    rep = N_HEADS // N_KV_HEADS
    qrow = lambda w: pl.BlockSpec((None, Q_TILE, w), lambda b, j: (b, j, 0))
    keys = lambda w: pl.BlockSpec((None, lk, w), lambda b, j: (b, 0, 0))
    kern = functools.partial(_dsa_kernel, seq_len=seq_len, top_k=top_k)
    return pl.pallas_call(
        kern,
        grid=(bsz, lp // Q_TILE),
        in_specs=[qrow(dq), qrow(qs.shape[-1]),
                  pl.BlockSpec((None, N_IDX_HEADS, Q_TILE), lambda b, j: (b, 0, j)),
                  keys(SPLIT_W), keys(k.shape[-1]),
                  pl.BlockSpec((None, N_KV_HEADS, HEAD_DIM, lk), lambda b, j: (b, 0, 0, 0))],
        out_specs=qrow(dq),
        out_shape=jax.ShapeDtypeStruct((bsz, lp, dq), BF16),
        scratch_shapes=[pltpu.VMEM((lk, Q_TILE), F32), pltpu.VMEM((lk, Q_TILE), BF16),
                        pltpu.VMEM((N_IDX_HEADS * Q_TILE, SPLIT_W), BF16),
                        pltpu.VMEM((N_KV_HEADS, rep * Q_TILE, 2 * HEAD_DIM), BF16),
                        pltpu.VMEM((N_KV_HEADS, HEAD_DIM, rep * Q_TILE), F32),
                        pltpu.VMEM((2, N_KV_HEADS, KEY_CHUNK, rep * Q_TILE), F32)],
        compiler_params=pltpu.CompilerParams(
            dimension_semantics=("parallel", "arbitrary"), vmem_limit_bytes=VMEM_LIMIT),
        name="dsa_attention",
    )(q, qs, w_t, kk, k, v_t)


def _merge_ffn_kernel(s_ref, ya_ref, ao_ref, gate_ref, wo_ref, wm_ref, fng_ref, up_ref, dww_ref,
                      dwb_ref, down_ref, fin_ref, out_ref, carry_ref, ubuf_ref, hb_ref, acc_ref):
    tm, d = s_ref.shape
    d_ff = down_ref.shape[0]
    i = pl.program_id(1)

    @pl.when(i == 0)
    def _():
        carry_ref[...] = jnp.zeros(carry_ref.shape, F32)

    y_b = jnp.dot(ao_ref[...], wo_ref[...], preferred_element_type=F32)
    mix = gate_ref[:, :d].astype(F32) * ya_ref[...].astype(F32) + gate_ref[:, d:].astype(F32) * y_b
    stream = s_ref[...] + jnp.dot(mix.astype(BF16), wm_ref[...], preferred_element_type=F32)
    h = stream * lax.rsqrt(jnp.mean(stream * stream, axis=-1, keepdims=True) + EPS) * fng_ref[...]
    hb_ref[...] = h.astype(BF16)
    acc_ref[...] = stream

    def conv3(u, col, slot):
        ubuf_ref[slot, 0:SUBLANES, :] = carry_ref[:, pl.ds(col, FFN_CHUNK)]
        ubuf_ref[slot, SUBLANES:, :] = u
        carry_ref[:, pl.ds(col, FFN_CHUNK)] = u[tm - SUBLANES:, :]
        w = dww_ref[:, pl.ds(col, FFN_CHUNK)]
        out = dwb_ref[:, pl.ds(col, FFN_CHUNK)] + w[2:3, :] * u
        out = out + w[1:2, :] * ubuf_ref[slot, SUBLANES - 1:SUBLANES - 1 + tm, :]
        return out + w[0:1, :] * ubuf_ref[slot, SUBLANES - 2:SUBLANES - 2 + tm, :]

    def chunk(c, carry):
        col_a = pl.multiple_of(c * FFN_CHUNK, FFN_CHUNK)
        col_b = pl.multiple_of(d_ff + c * FFN_CHUNK, FFN_CHUNK)
        hb = hb_ref[...]
        a = conv3(jnp.dot(hb, up_ref[:, pl.ds(col_a, FFN_CHUNK)], preferred_element_type=F32), col_a, 0)
        b = conv3(jnp.dot(hb, up_ref[:, pl.ds(col_b, FFN_CHUNK)], preferred_element_type=F32), col_b, 1)
        act = (a * jax.nn.sigmoid(a) * b).astype(BF16)
        acc_ref[...] += jnp.dot(act, down_ref[pl.ds(col_a, FFN_CHUNK), :], preferred_element_type=F32)
        return carry

    lax.fori_loop(0, d_ff // FFN_CHUNK, chunk, 0, unroll=True)

    y = acc_ref[...]
    out_ref[...] = y * lax.rsqrt(jnp.mean(y * y, axis=-1, keepdims=True) + EPS) * fin_ref[...]


def _merge_ffn(stream, ya, ao, gates, wo, wm, fng, up, dww, dwb, down, fin, tm):
    bsz, lp, d = stream.shape
    row = lambda w: pl.BlockSpec((None, tm, w), lambda b, i: (b, i, 0))
    full = lambda a: pl.BlockSpec(a.shape, lambda b, i: (0,) * a.ndim)
    return pl.pallas_call(
        _merge_ffn_kernel,
        grid=(bsz, lp // tm),
        in_specs=[row(d), row(d), row(d), row(2 * d), full(wo), full(wm), full(fng), full(up),
                  full(dww), full(dwb), full(down), full(fin)],
        out_specs=row(d),
        out_shape=jax.ShapeDtypeStruct((bsz, lp, d), F32),
        scratch_shapes=[pltpu.VMEM((SUBLANES, up.shape[1]), F32),
                        pltpu.VMEM((2, SUBLANES + tm, FFN_CHUNK), F32),
                        pltpu.VMEM((tm, d), BF16), pltpu.VMEM((tm, d), F32)],
        compiler_params=pltpu.CompilerParams(
            dimension_semantics=("parallel", "arbitrary"), vmem_limit_bytes=VMEM_LIMIT),
        name="merge_ffn",
    )(stream, ya, ao, gates, wo, wm, fng, up, dww, dwb, down, fin)


def _row_tile(lp):
    for t in (ROW_TILE, 256, LANES):
        if lp % t == 0:
            return t
    raise ValueError(f"padded length {lp} is not a multiple of {LANES}")


def _split_bf16(w):
    hi = w.astype(BF16)
    return hi, (w - hi.astype(F32)).astype(BF16)


def kernel(x, meta_tokens, mix_norm_g, w_in, conv_ln_g, conv_ln_b, conv_dw_w, conv_dw_b, conv_pw_out,
           attn_w_o, w_merge_out, ffn_norm_g, ffn_up, ffn_dw_w, ffn_dw_b, ffn_down, final_norm_g):
    bsz, seq, d = x.shape
    depth = w_in.shape[0]
    seq_len = seq + N_META
    lp = -(-seq_len // LANES) * LANES
    lk = -(-lp // KEY_CHUNK) * KEY_CHUNK
    tm = _row_tile(lp)
    top_k = min(TOPK_MAX, seq_len // 4)
    assert top_k > N_META and conv_dw_w.shape[1] == CONV_WIDTH and ffn_dw_w.shape[1] == FFN_CONV_WIDTH
    d_conv = conv_pw_out.shape[1]
    d_ff = ffn_down.shape[1]
    assert d_ff % FFN_CHUNK == 0 and d % LANES == 0

    meta = jnp.broadcast_to(meta_tokens.astype(x.dtype)[None], (bsz, N_META, d))
    stream = jnp.concatenate([meta, x, jnp.zeros((bsz, lp - seq_len, d), x.dtype)], axis=1)
    tabs = _rope_tables(lp, HEAD_DIM, 1) + _rope_tables(lp, IDX_DIM, LANES // IDX_DIM)

    n_q, n_kv = N_HEADS * HEAD_DIM, N_KV_HEADS * HEAD_DIM
    n_qi = N_IDX_HEADS * IDX_DIM
    edges = [0, 2 * d_conv]
    for wdt in (n_q, n_kv, n_kv, n_qi, IDX_DIM, N_IDX_HEADS, 2 * d):
        edges.append(edges[-1] + wdt)
    row2 = lambda a: a.reshape(1, -1).astype(F32)
    pad_keys = lambda a: jnp.pad(a, ((0, 0), (0, lk - lp), (0, 0)))

    for l in range(depth):
        w = w_in[l]
        wu = w[:, edges[0]:edges[1]].astype(BF16)
        wq = (w[:, edges[1]:edges[2]] * (HEAD_DIM ** -0.5 * math.log2(math.e))).astype(BF16)
        wkv = w[:, edges[2]:edges[4]].astype(BF16)
        wqih, wqil = _split_bf16(w[:, edges[4]:edges[5]])
        wkwh, wkwl = _split_bf16(
            jnp.pad(w[:, edges[5]:edges[7]], ((0, 0), (0, LANES - IDX_DIM - N_IDX_HEADS))))
        wg = w[:, edges[7]:edges[8]].astype(BF16)

        glu, q, k, v, qs, kk, w_idx, gates = _in_proj(
            stream, row2(mix_norm_g[l]), wu, wq, wkv, wqih, wqil, wkwh, wkwl, wg, tabs, tm)
        ya = _conv_branch(glu, conv_dw_w[l], row2(conv_dw_b[l]), row2(conv_ln_g[l]), row2(conv_ln_b[l]),
                          conv_pw_out[l].astype(BF16), tm)
        v_t = jnp.swapaxes(pad_keys(v).reshape(bsz, lk, N_KV_HEADS, HEAD_DIM), 1, 3).swapaxes(1, 2)
        ao = _dsa(q, qs, jnp.swapaxes(w_idx, 1, 2), pad_keys(kk), pad_keys(k), v_t, seq_len, top_k)
        last = l == depth - 1
        stream = _merge_ffn(stream, ya, ao, gates, attn_w_o[l].astype(BF16), w_merge_out[l].astype(BF16),
                            row2(ffn_norm_g[l]), ffn_up[l].astype(BF16), ffn_dw_w[l], row2(ffn_dw_b[l]),
                            ffn_down[l].astype(BF16), row2(final_norm_g), tm)
        assert last, "the fused final norm assumes a single layer"
    return stream[:, N_META:seq_len]
```

```python
import functools
import math

import jax
import jax.numpy as jnp
from jax import lax
from jax.experimental import pallas as pl
from jax.experimental.pallas import tpu as pltpu

N_META = 16
CONV_WIDTH = 31
N_HEADS = 8
HEAD_DIM = 128
N_KV_HEADS = 2
N_IDX_HEADS = 8
IDX_DIM = 64
TOPK_MAX = 256
FFN_CONV_WIDTH = 3
ROPE_THETA = 10000.0
EPS = 1e-6

LANES = 128
SUBLANES = 8
Q_TILE = 128
KEY_CHUNK = 512
ROW_TILE = 384
CONV_HALO = 32
CONV_ROWS = 64
FFN_CHUNK = 256
SPLIT_W = 4 * IDX_DIM
SAMPLE_STRIDE = 8
FALSI_FIRST_RUN = 11
FALSI_RUN = 2
MAX_FALSI_STEPS = 25
VMEM_LIMIT = 56 * 1024 * 1024

F32 = jnp.float32
BF16 = jnp.bfloat16
NEG_INF = float("-inf")
MASK_BIAS = -1e30
F32_LOWEST = float(jnp.finfo(jnp.float32).min)
NT_DIMS = (((1,), (1,)), ((), ()))


def _rope_tables(n_pos, dim, reps):
    half = dim // 2
    inv = ROPE_THETA ** (-jnp.arange(half, dtype=F32) / half)
    ang = jnp.arange(n_pos, dtype=jnp.int32).astype(F32)[:, None] * inv[None, :]
    cos, sin = jnp.cos(ang), jnp.sin(ang)
    cos_f = jnp.concatenate([cos, cos], axis=-1)
    sin_f = jnp.concatenate([-sin, sin], axis=-1)
    return jnp.tile(cos_f, (1, reps)), jnp.tile(sin_f, (1, reps))


def _rope128(x, cos, sin):
    return x * cos + pltpu.roll(x, HEAD_DIM // 2, axis=1) * sin


def _rope64(x, cos, sin, first_half):
    partner = jnp.where(first_half, pltpu.roll(x, LANES - IDX_DIM // 2, axis=1),
                        pltpu.roll(x, IDX_DIM // 2, axis=1))
    return x * cos + partner * sin


def _hi_lo(x):
    hi = x.astype(BF16).astype(F32)
    return hi, x - hi


def _in_proj_kernel(x_ref, g_ref, wu_ref, wq_ref, wkv_ref, wqih_ref, wqil_ref, wkwh_ref, wkwl_ref, wg_ref,
                    c128_ref, s128_ref, c64_ref, s64_ref,
                    glu_ref, q_ref, k_ref, v_ref, qs_ref, kk_ref, w_ref, gate_ref):
    d_conv = glu_ref.shape[-1]
    kv_w = k_ref.shape[-1]
    x = x_ref[...]
    h = x * lax.rsqrt(jnp.mean(x * x, axis=-1, keepdims=True) + EPS) * g_ref[...]
    hb = h.astype(BF16)
    hl = (h - hb.astype(F32)).astype(BF16)

    u = jnp.dot(hb, wu_ref[...], preferred_element_type=F32)
    glu_ref[...] = (u[:, :d_conv] * jax.nn.sigmoid(u[:, d_conv:])).astype(glu_ref.dtype)

    gate_ref[...] = jax.nn.sigmoid(
        jnp.dot(hb, wg_ref[...], preferred_element_type=F32)).astype(gate_ref.dtype)

    c128, s128 = c128_ref[...], s128_ref[...]
    q = jnp.dot(hb, wq_ref[...], preferred_element_type=F32)
    for hd in range(q.shape[-1] // HEAD_DIM):
        sl = slice(hd * HEAD_DIM, (hd + 1) * HEAD_DIM)
        q_ref[:, sl] = _rope128(q[:, sl], c128, s128).astype(q_ref.dtype)

    kv = jnp.dot(hb, wkv_ref[...], preferred_element_type=F32)
    for hd in range(kv_w // HEAD_DIM):
        sl = slice(hd * HEAD_DIM, (hd + 1) * HEAD_DIM)
        k_ref[:, sl] = _rope128(kv[:, sl], c128, s128).astype(k_ref.dtype)
    v_ref[...] = kv[:, kv_w:].astype(v_ref.dtype)

    def split_dot(w_hi_ref, w_lo_ref):
        w_hi = w_hi_ref[...]
        return (jnp.dot(hb, w_hi, preferred_element_type=F32) + jnp.dot(hl, w_hi, preferred_element_type=F32)
                + jnp.dot(hb, w_lo_ref[...], preferred_element_type=F32))

    c64, s64 = c64_ref[...], s64_ref[...]
    lane = lax.broadcasted_iota(jnp.int32, (x.shape[0], LANES), 1)
    first_half = (lane & (IDX_DIM - 1)) < (IDX_DIM // 2)
    low_head = lane < IDX_DIM

    qi = split_dot(wqih_ref, wqil_ref)
    for blk in range(qi.shape[-1] // LANES):
        xr = _rope64(qi[:, blk * LANES:(blk + 1) * LANES], c64, s64, first_half)
        xs = pltpu.roll(xr, IDX_DIM, axis=1)
        for sub, dup in enumerate((jnp.where(low_head, xr, xs), jnp.where(low_head, xs, xr))):
            hi, lo = _hi_lo(dup)
            c0 = (2 * blk + sub) * SPLIT_W
            qs_ref[:, c0:c0 + LANES] = hi.astype(BF16)
            qs_ref[:, c0 + LANES:c0 + 2 * LANES] = lo.astype(BF16)

    kw = split_dot(wkwh_ref, wkwl_ref)
    kr = _rope64(kw, c64, s64, first_half)
    hi, lo = _hi_lo(jnp.where(low_head, kr, pltpu.roll(kr, IDX_DIM, axis=1)))
    mix = jnp.where(low_head, hi, lo).astype(BF16)
    kk_ref[:, 0:LANES] = mix
    kk_ref[:, LANES:2 * LANES] = mix
    w_ref[...] = kw[:, IDX_DIM:IDX_DIM + N_IDX_HEADS] * (N_IDX_HEADS ** -0.5 * IDX_DIM ** -0.5)


def _in_proj(stream, g, wu, wq, wkv, wqih, wqil, wkwh, wkwl, wg, tabs, tm):
    bsz, lp, d = stream.shape
    nt = lp // tm
    row = lambda w: pl.BlockSpec((None, tm, w), lambda b, i: (b, i, 0))
    full = lambda a: pl.BlockSpec(a.shape, lambda b, i: (0,) * a.ndim)
    tab = pl.BlockSpec((tm, LANES), lambda b, i: (i, 0))
    outs = [(wu.shape[1] // 2, BF16), (wq.shape[1], BF16), (wkv.shape[1] // 2, BF16),
            (wkv.shape[1] // 2, BF16), (N_IDX_HEADS * SPLIT_W, BF16), (SPLIT_W, BF16),
            (N_IDX_HEADS, F32), (wg.shape[1], BF16)]
    return pl.pallas_call(
        _in_proj_kernel,
        grid=(bsz, nt),
        in_specs=[row(d), full(g), full(wu), full(wq), full(wkv), full(wqih), full(wqil), full(wkwh),
                  full(wkwl), full(wg), tab, tab, tab, tab],
        out_specs=[row(w) for w, _ in outs],
        out_shape=[jax.ShapeDtypeStruct((bsz, lp, w), dt) for w, dt in outs],
        compiler_params=pltpu.CompilerParams(
            dimension_semantics=("parallel", "parallel"), vmem_limit_bytes=VMEM_LIMIT),
        name="in_proj",
    )(stream, g, wu, wq, wkv, wqih, wqil, wkwh, wkwl, wg, *tabs)


def _conv_kernel(glu_ref, dww_ref, dwb_ref, lng_ref, lnb_ref, pw_ref, ya_ref, buf_ref, conv_ref):
    tt, c = glu_ref.shape
    i = pl.program_id(1)

    @pl.when(i == 0)
    def _():
        buf_ref[0:CONV_HALO, :] = jnp.zeros((CONV_HALO, c), F32)

    @pl.when(i > 0)
    def _():
        buf_ref[0:CONV_HALO, :] = buf_ref[tt:tt + CONV_HALO, :]

    buf_ref[CONV_HALO:, :] = glu_ref[...].astype(F32)

    base = CONV_HALO - (CONV_WIDTH - 1)
    for cb in range(c // LANES):
        cs = slice(cb * LANES, (cb + 1) * LANES)
        w_c = dww_ref[:, cs]
        b_c = dwb_ref[:, cs]
        for rb in range(tt // CONV_ROWS):
            r0 = rb * CONV_ROWS
            acc = jnp.broadcast_to(b_c, (CONV_ROWS, LANES))
            for ph in range(SUBLANES):
                rows = CONV_ROWS + (SUBLANES if ph else 0)
                part = None
                for a in range((base + CONV_WIDTH - 1) // SUBLANES + 1):
                    j = SUBLANES * a + ph - base
                    if 0 <= j < CONV_WIDTH:
                        term = buf_ref[r0 + SUBLANES * a:r0 + SUBLANES * a + rows, cs] * w_c[j:j + 1, :]
                        part = term if part is None else part + term
                acc = acc + part[ph:ph + CONV_ROWS, :]
            conv_ref[r0:r0 + CONV_ROWS, cs] = acc

    hc = conv_ref[...]
    mu = jnp.mean(hc, axis=-1, keepdims=True)
    dlt = hc - mu
    var = jnp.mean(dlt * dlt, axis=-1, keepdims=True)
    hn = dlt * lax.rsqrt(var + EPS) * lng_ref[...] + lnb_ref[...]
    hs = hn * jax.nn.sigmoid(hn)
    ya_ref[...] = jnp.dot(hs.astype(BF16), pw_ref[...], preferred_element_type=F32).astype(ya_ref.dtype)


def _conv_branch(glu, dww, dwb, lng, lnb, pw, tt):
    bsz, lp, c = glu.shape
    full = lambda a: pl.BlockSpec(a.shape, lambda b, i: (0,) * a.ndim)
    return pl.pallas_call(
        _conv_kernel,
        grid=(bsz, lp // tt),
        in_specs=[pl.BlockSpec((None, tt, c), lambda b, i: (b, i, 0)),
                  full(dww), full(dwb), full(lng), full(lnb), full(pw)],
        out_specs=pl.BlockSpec((None, tt, pw.shape[1]), lambda b, i: (b, i, 0)),
        out_shape=jax.ShapeDtypeStruct((bsz, lp, pw.shape[1]), BF16),
        scratch_shapes=[pltpu.VMEM((CONV_HALO + tt, c), F32), pltpu.VMEM((tt, c), F32)],
        compiler_params=pltpu.CompilerParams(
            dimension_semantics=("parallel", "arbitrary"), vmem_limit_bytes=VMEM_LIMIT),
        name="conv_branch",
    )(glu, dww, dwb, lng, lnb, pw)


def _fold_rows(x, op):
    parts = [x[r * SUBLANES:(r + 1) * SUBLANES, :] for r in range(x.shape[0] // SUBLANES)]
    while len(parts) > 1:
        nxt = [op(parts[i], parts[i + 1]) for i in range(0, len(parts) - 1, 2)]
        parts = nxt + parts[len(parts) - len(parts) % 2:]
    return parts[0]


def _count(st_ref, thr, n_chunks, strict):
    tq = st_ref.shape[1]

    def body(c, acc):
        off = pl.multiple_of(c * KEY_CHUNK, KEY_CHUNK)
        blk = st_ref[pl.ds(off, KEY_CHUNK), :]
        hit = (blk > thr) if strict else (blk >= thr)
        return acc + _fold_rows(jnp.where(hit, 1.0, 0.0), jnp.add)

    acc = lax.fori_loop(0, n_chunks, body, jnp.zeros((SUBLANES, tq), F32))
    return jnp.sum(acc, axis=0, keepdims=True)


def _count_above(st_ref, thr, n_chunks):
    return _count(st_ref, thr, n_chunks, True)


def _count_at_least(st_ref, thr, n_chunks):
    return _count(st_ref, thr, n_chunks, False)


def _min_above(st_ref, lo, n_chunks):
    tq = st_ref.shape[1]

    def body(c, acc):
        off = pl.multiple_of(c * KEY_CHUNK, KEY_CHUNK)
        blk = st_ref[pl.ds(off, KEY_CHUNK), :]
        return jnp.minimum(acc, _fold_rows(jnp.where(blk > lo, blk, jnp.inf), jnp.minimum))

    acc = lax.fori_loop(0, n_chunks, body, jnp.full((SUBLANES, tq), jnp.inf, F32))
    return jnp.min(acc, axis=0, keepdims=True)


def _bracket_ends(st_ref, lo, hi, n_chunks):
    tq = st_ref.shape[1]

    def body(c, carry):
        bot, top = carry
        off = pl.multiple_of(c * KEY_CHUNK, KEY_CHUNK)
        blk = st_ref[pl.ds(off, KEY_CHUNK), :]
        bot = jnp.minimum(bot, _fold_rows(jnp.where(blk > lo, blk, jnp.inf), jnp.minimum))
        top = jnp.maximum(top, _fold_rows(jnp.where(blk <= hi, blk, NEG_INF), jnp.maximum))
        return bot, top

    bot, top = lax.fori_loop(0, n_chunks, body, (jnp.full((SUBLANES, tq), jnp.inf, F32),
                                                jnp.full((SUBLANES, tq), NEG_INF, F32)))
    return jnp.min(bot, axis=0, keepdims=True), jnp.max(top, axis=0, keepdims=True)


def _dsa_kernel(q_ref, qs_ref, wt_ref, kk_ref, k_ref, vt_ref, o_ref,
                st_ref, bias_ref, qsr_ref, qaug_ref, acc_ref, lg_ref, *, seq_len, top_k):
    tq = q_ref.shape[0]
    j = pl.program_id(1)
    q0 = j * tq
    n_chunks = (q0 + tq + KEY_CHUNK - 1) // KEY_CHUNK
    rep = N_HEADS // N_KV_HEADS
    k_sel = float(top_k)
    q_pos = q0 + lax.broadcasted_iota(jnp.int32, (1, tq), 1)

    for hd in range(N_IDX_HEADS):
        qsr_ref[hd * tq:(hd + 1) * tq, :] = qs_ref[:, hd * SPLIT_W:(hd + 1) * SPLIT_W]
    eye = (lax.broadcasted_iota(jnp.int32, (tq, tq), 0)
           == lax.broadcasted_iota(jnp.int32, (tq, tq), 1)).astype(BF16)
    for g in range(N_KV_HEADS):
        for r in range(rep):
            hd = g * rep + r
            qaug_ref[g, r * tq:(r + 1) * tq, 0:HEAD_DIM] = q_ref[:, hd * HEAD_DIM:(hd + 1) * HEAD_DIM]
            qaug_ref[g, r * tq:(r + 1) * tq, HEAD_DIM:] = eye

    w_t = wt_ref[...]

    def score_body(i, carry):
        mn, mx, s1, s2 = carry
        for u in range(2):
            c = jnp.minimum(2 * i + u, n_chunks - 1)
            off = pl.multiple_of(c * KEY_CHUNK, KEY_CHUNK)
            dots = lax.dot_general(kk_ref[pl.ds(off, KEY_CHUNK), :], qsr_ref[...], NT_DIMS,
                                   preferred_element_type=F32)
            sc = w_t[0:1, :] * jnp.maximum(dots[:, 0:tq], 0.0)
            for hd in range(1, N_IDX_HEADS):
                sc = sc + w_t[hd:hd + 1, :] * jnp.maximum(dots[:, hd * tq:(hd + 1) * tq], 0.0)
            st_ref[pl.ds(off, KEY_CHUNK), :] = sc
            mn = jnp.minimum(mn, _fold_rows(sc, jnp.minimum))
            mx = jnp.maximum(mx, _fold_rows(sc, jnp.maximum))
            fresh = jnp.where(2 * i + u < n_chunks, 1.0, 0.0)
            for r in range(0, KEY_CHUNK, SAMPLE_STRIDE * SUBLANES):
                smp = sc[r:r + SUBLANES, :]
                s1 = s1 + fresh * smp
                s2 = s2 + fresh * (smp * smp)
        return mn, mx, s1, s2

    zeros8 = jnp.zeros((SUBLANES, tq), F32)
    mn, mx, s1, s2 = lax.fori_loop(
        0, (n_chunks + 1) // 2, score_body,
        (jnp.full((SUBLANES, tq), jnp.inf, F32), jnp.full((SUBLANES, tq), NEG_INF, F32), zeros8, zeros8))
    row_min = jnp.min(mn, axis=0, keepdims=True)
    row_max = jnp.max(mx, axis=0, keepdims=True)
    n_smp = (n_chunks * (KEY_CHUNK // SAMPLE_STRIDE)).astype(F32)
    smp_mean = jnp.sum(s1, axis=0, keepdims=True) / n_smp
    smp_var = jnp.maximum(jnp.sum(s2, axis=0, keepdims=True) / n_smp - smp_mean * smp_mean, 0.0)

    meta_pos = lax.broadcasted_iota(jnp.int32, (N_META, tq), 0)
    st_ref[0:N_META, :] = jnp.where(meta_pos <= q_pos, jnp.inf, NEG_INF)

    def causal_body(blk, carry):
        off = pl.multiple_of(blk * tq, tq)
        key_pos = off + lax.broadcasted_iota(jnp.int32, (tq, tq), 0)
        keep = (key_pos <= q_pos) & (key_pos < seq_len)
        st_ref[pl.ds(off, tq), :] = jnp.where(keep, st_ref[pl.ds(off, tq), :], NEG_INF)
        return carry

    lax.fori_loop(j, n_chunks * (KEY_CHUNK // tq), causal_body, 0)

    select_all = ((q_pos + 1) <= top_k) | (q_pos >= seq_len)

    def search():
        lo0 = jnp.where(select_all, 0.0, jnp.minimum(row_min, 0.0) * 2.0 - 1.0)
        hi0 = jnp.where(select_all, 0.0, row_max)

        def bisect(_, carry):
            lo, hi = carry
            mid = 0.5 * lo + 0.5 * hi
            ge = _count_above(st_ref, mid, n_chunks) >= k_sel
            return jnp.where(ge, mid, lo), jnp.where(ge, hi, mid)

        def snap(lo, hi):
            cand = _min_above(st_ref, lo, n_chunks)
            done = select_all | (_count_above(st_ref, cand, n_chunks) < k_sel)
            return jnp.where(done, lo, cand), hi, cand, jnp.sum(jnp.where(done, 0.0, 1.0))

        def more(state):
            lo, hi = lax.fori_loop(0, 4, bisect, state[:2])
            return snap(lo, hi)

        n_vis = jnp.minimum(q_pos + 1, seq_len).astype(F32)
        n_meta = jnp.minimum(q_pos + 1, N_META).astype(F32)
        tail = jnp.clip((k_sel - n_meta - 0.5) / jnp.maximum(n_vis - n_meta, 1.0), 1e-6, 1.0 - 1e-6)
        tt = jnp.sqrt(-2.0 * jnp.log(jnp.minimum(tail, 1.0 - tail)))
        z = tt - ((0.010328 * tt + 0.802853) * tt + 2.515517) / (
            ((0.001308 * tt + 0.189269) * tt + 1.432788) * tt + 1.0)
        guess = smp_mean + jnp.where(tail < 0.5, z, -z) * jnp.sqrt(smp_var)
        target = k_sel - 0.5

        def wide(c_lo, c_hi):
            return jnp.sum(jnp.where(jnp.logical_not(select_all) & (c_lo - c_hi > 2.0), 1.0, 0.0))

        def falsi(state):
            lo, hi, c_lo, c_hi, f_lo, f_hi, side, it = state
            t = jnp.where(it == 0, guess, lo + (hi - lo) * (f_lo / (f_lo - f_hi)))
            t = jnp.where((t > lo) & (t < hi), t, 0.5 * lo + 0.5 * hi)
            c = _count_above(st_ref, t, n_chunks)
            ge = c >= k_sel
            f = c - target
            f_lo = jnp.where(ge, f, jnp.where(side < 0.0, 0.5 * f_lo, f_lo))
            f_hi = jnp.where(ge, jnp.where(side > 0.0, 0.5 * f_hi, f_hi), f)
            lo, c_lo = jnp.where(ge, t, lo), jnp.where(ge, c, c_lo)
            hi, c_hi = jnp.where(ge, hi, t), jnp.where(ge, c_hi, c)
            return lo, hi, c_lo, c_hi, f_lo, f_hi, jnp.where(ge, 1.0, -1.0), it + 1

        def falsi_run(steps, state):
            state = lax.fori_loop(0, steps, lambda _, s: falsi(s), state)
            return state, wide(state[2], state[3])

        c_lo0, c_hi0 = n_vis, n_meta
        state = (lo0, hi0, c_lo0, c_hi0, c_lo0 - target, c_hi0 - target, jnp.zeros((1, tq), F32),
                 jnp.int32(0))
        state, n_wide = lax.while_loop(
            lambda s: (s[0][7] < MAX_FALSI_STEPS) & (s[1] > 0.0),
            lambda s: falsi_run(FALSI_RUN, s[0]), falsi_run(FALSI_FIRST_RUN, state))
        lo, hi, c_lo, c_hi = state[:4]
        bot, top = _bracket_ends(st_ref, lo, hi, n_chunks)
        is_top = (k_sel - c_hi) <= 1.0
        single = bot == top
        kth2 = jnp.where(is_top, top, bot)
        n_gt2 = jnp.where(single | is_top, c_hi, c_hi + 1.0)
        n_ge2 = jnp.where(single | jnp.logical_not(is_top), c_lo, c_hi + 1.0)

        def robust():
            narrow = select_all | (c_lo - c_hi <= 1.0)
            done = narrow | (_count_above(st_ref, bot, n_chunks) < k_sel)
            first = (jnp.where(done, lo, bot), hi, bot, jnp.sum(jnp.where(done, 0.0, 1.0)))
            kth = lax.while_loop(lambda s: s[3] > 0.0, more, first)[2]
            return kth, _count_above(st_ref, kth, n_chunks), _count_at_least(st_ref, kth, n_chunks)

        kth, n_gt, n_ge = lax.cond(n_wide > 0.0, robust, lambda: (kth2, n_gt2, n_ge2))

        thr = jnp.where(select_all, F32_LOWEST, kth)
        eq_take = jnp.where(select_all | (n_ge <= k_sel), jnp.inf, k_sel - n_gt)
        return thr, eq_take

    def keep_all():
        return jnp.full((1, tq), F32_LOWEST, F32), jnp.full((1, tq), jnp.inf, F32)

    thr, eq_take = lax.cond(q0 + tq > top_k, search, keep_all)
    any_tie = jnp.max(jnp.where(eq_take < jnp.inf, 1.0, 0.0)) > 0.0

    @pl.when(jnp.logical_not(any_tie))
    def _():
        def bias_body(c, carry):
            off = pl.multiple_of(c * KEY_CHUNK, KEY_CHUNK)
            sel = st_ref[pl.ds(off, KEY_CHUNK), :] >= thr
            bias_ref[pl.ds(off, KEY_CHUNK), :] = jnp.where(sel, 0.0, MASK_BIAS).astype(BF16)
            return carry

        lax.fori_loop(0, n_chunks, bias_body, 0)

    @pl.when(any_tie)
    def _():
        tri = (lax.broadcasted_iota(jnp.int32, (KEY_CHUNK, KEY_CHUNK), 1)
               <= lax.broadcasted_iota(jnp.int32, (KEY_CHUNK, KEY_CHUNK), 0)).astype(BF16)

        def tie_body(c, seen):
            off = pl.multiple_of(c * KEY_CHUNK, KEY_CHUNK)
            sc = st_ref[pl.ds(off, KEY_CHUNK), :]
            eq = jnp.where(sc == thr, 1.0, 0.0)
            rank = seen + jnp.dot(tri, eq.astype(BF16), preferred_element_type=F32)
            sel = (sc > thr) | ((sc == thr) & (rank <= eq_take))
            bias_ref[pl.ds(off, KEY_CHUNK), :] = jnp.where(sel, 0.0, MASK_BIAS).astype(BF16)
            return seen + jnp.sum(eq, axis=0, keepdims=True)

        lax.fori_loop(0, n_chunks, tie_body, jnp.zeros((1, tq), F32))

    acc_ref[...] = jnp.zeros(acc_ref.shape, F32)

    def qk(c, slot):
        off = pl.multiple_of(c * KEY_CHUNK, KEY_CHUNK)
        bias = bias_ref[pl.ds(off, KEY_CHUNK), :]
        for g in range(N_KV_HEADS):
            k_g = k_ref[pl.ds(off, KEY_CHUNK), g * HEAD_DIM:(g + 1) * HEAD_DIM]
            lg_ref[slot, g] = lax.dot_general(jnp.concatenate([k_g, bias], axis=1), qaug_ref[g], NT_DIMS,
                                              preferred_element_type=F32)

    def softmax_pv(c, slot, stats):
        off = pl.multiple_of(c * KEY_CHUNK, KEY_CHUNK)
        out = []
        for g in range(N_KV_HEADS):
            m_old, l_old = stats[2 * g], stats[2 * g + 1]
            logits = lg_ref[slot, g]
            m_new = jnp.maximum(m_old, jnp.max(logits, axis=0, keepdims=True))
            p = jnp.exp2(logits - m_new)
            alpha = jnp.exp2(m_old - m_new)
            l_new = alpha * l_old + jnp.sum(p, axis=0, keepdims=True)
            pv = jnp.dot(vt_ref[g, :, pl.ds(off, KEY_CHUNK)], p.astype(BF16), preferred_element_type=F32)
            acc_ref[g] = alpha * acc_ref[g] + pv
            out += [m_new, l_new]
        return tuple(out)

    last = n_chunks - 1
    qk(0, 0)

    def attn_body(i, stats):
        c = 2 * i
        qk(jnp.minimum(c + 1, last), 1)
        stats = softmax_pv(c, 0, stats)

        def second_half(stats):
            qk(jnp.minimum(c + 2, last), 0)
            return softmax_pv(c + 1, 1, stats)

        return lax.cond(c + 1 < n_chunks, second_half, lambda s: s, stats)

    init = (jnp.full((1, rep * tq), NEG_INF, F32), jnp.zeros((1, rep * tq), F32)) * N_KV_HEADS
    stats = lax.fori_loop(0, (n_chunks + 1) // 2, attn_body, init)

    for g in range(N_KV_HEADS):
        o_t = acc_ref[g] / stats[2 * g + 1]
        for r in range(rep):
            hd = g * rep + r
            o_ref[:, hd * HEAD_DIM:(hd + 1) * HEAD_DIM] = o_t[:, r * tq:(r + 1) * tq].T.astype(o_ref.dtype)


def _dsa(q, qs, w_t, kk, k, v_t, seq_len, top_k):
    bsz, lp, dq = q.shape
    lk = kk.shape[1]
    rep = N_HEADS // N_KV_HEADS
    qrow = lambda w: pl.BlockSpec((None, Q_TILE, w), lambda b, j: (b, j, 0))
    keys = lambda w: pl.BlockSpec((None, lk, w), lambda b, j: (b, 0, 0))
    kern = functools.partial(_dsa_kernel, seq_len=seq_len, top_k=top_k)
    return pl.pallas_call(
        kern,
        grid=(bsz, lp // Q_TILE),
        in_specs=[qrow(dq), qrow(qs.shape[-1]),
                  pl.BlockSpec((None, N_IDX_HEADS, Q_TILE), lambda b, j: (b, 0, j)),
                  keys(SPLIT_W), keys(k.shape[-1]),
                  pl.BlockSpec((None, N_KV_HEADS, HEAD_DIM, lk), lambda b, j: (b, 0, 0, 0))],
        out_specs=qrow(dq),
        out_shape=jax.ShapeDtypeStruct((bsz, lp, dq), BF16),
        scratch_shapes=[pltpu.VMEM((lk, Q_TILE), F32), pltpu.VMEM((lk, Q_TILE), BF16),
                        pltpu.VMEM((N_IDX_HEADS * Q_TILE, SPLIT_W), BF16),
                        pltpu.VMEM((N_KV_HEADS, rep * Q_TILE, 2 * HEAD_DIM), BF16),
                        pltpu.VMEM((N_KV_HEADS, HEAD_DIM, rep * Q_TILE), F32),
                        pltpu.VMEM((2, N_KV_HEADS, KEY_CHUNK, rep * Q_TILE), F32)],
        compiler_params=pltpu.CompilerParams(
            dimension_semantics=("parallel", "arbitrary"), vmem_limit_bytes=VMEM_LIMIT),
        name="dsa_attention",
    )(q, qs, w_t, kk, k, v_t)


def _merge_ffn_kernel(s_ref, ya_ref, ao_ref, gate_ref, wo_ref, wm_ref, fng_ref, up_ref, dww_ref,
                      dwb_ref, down_ref, fin_ref, out_ref, carry_ref, ubuf_ref, hb_ref, acc_ref):
    tm, d = s_ref.shape
    d_ff = down_ref.shape[0]
    i = pl.program_id(1)

    @pl.when(i == 0)
    def _():
        carry_ref[...] = jnp.zeros(carry_ref.shape, F32)

    y_b = jnp.dot(ao_ref[...], wo_ref[...], preferred_element_type=F32)
    mix = gate_ref[:, :d].astype(F32) * ya_ref[...].astype(F32) + gate_ref[:, d:].astype(F32) * y_b
    stream = s_ref[...] + jnp.dot(mix.astype(BF16), wm_ref[...], preferred_element_type=F32)
    h = stream * lax.rsqrt(jnp.mean(stream * stream, axis=-1, keepdims=True) + EPS) * fng_ref[...]
    hb_ref[...] = h.astype(BF16)
    acc_ref[...] = stream

    def conv3(u, col, slot):
        ubuf_ref[slot, 0:SUBLANES, :] = carry_ref[:, pl.ds(col, FFN_CHUNK)]
        ubuf_ref[slot, SUBLANES:, :] = u
        carry_ref[:, pl.ds(col, FFN_CHUNK)] = u[tm - SUBLANES:, :]
        w = dww_ref[:, pl.ds(col, FFN_CHUNK)]
        out = dwb_ref[:, pl.ds(col, FFN_CHUNK)] + w[2:3, :] * u
        out = out + w[1:2, :] * ubuf_ref[slot, SUBLANES - 1:SUBLANES - 1 + tm, :]
        return out + w[0:1, :] * ubuf_ref[slot, SUBLANES - 2:SUBLANES - 2 + tm, :]

    def chunk(c, carry):
        col_a = pl.multiple_of(c * FFN_CHUNK, FFN_CHUNK)
        col_b = pl.multiple_of(d_ff + c * FFN_CHUNK, FFN_CHUNK)
        hb = hb_ref[...]
        a = conv3(jnp.dot(hb, up_ref[:, pl.ds(col_a, FFN_CHUNK)], preferred_element_type=F32), col_a, 0)
        b = conv3(jnp.dot(hb, up_ref[:, pl.ds(col_b, FFN_CHUNK)], preferred_element_type=F32), col_b, 1)
        act = (a * jax.nn.sigmoid(a) * b).astype(BF16)
        acc_ref[...] += jnp.dot(act, down_ref[pl.ds(col_a, FFN_CHUNK), :], preferred_element_type=F32)
        return carry

    lax.fori_loop(0, d_ff // FFN_CHUNK, chunk, 0, unroll=True)

    y = acc_ref[...]
    out_ref[...] = y * lax.rsqrt(jnp.mean(y * y, axis=-1, keepdims=True) + EPS) * fin_ref[...]


def _merge_ffn(stream, ya, ao, gates, wo, wm, fng, up, dww, dwb, down, fin, tm):
    bsz, lp, d = stream.shape
    row = lambda w: pl.BlockSpec((None, tm, w), lambda b, i: (b, i, 0))
    full = lambda a: pl.BlockSpec(a.shape, lambda b, i: (0,) * a.ndim)
    return pl.pallas_call(
        _merge_ffn_kernel,
        grid=(bsz, lp // tm),
        in_specs=[row(d), row(d), row(d), row(2 * d), full(wo), full(wm), full(fng), full(up),
                  full(dww), full(dwb), full(down), full(fin)],
        out_specs=row(d),
        out_shape=jax.ShapeDtypeStruct((bsz, lp, d), F32),
        scratch_shapes=[pltpu.VMEM((SUBLANES, up.shape[1]), F32),
                        pltpu.VMEM((2, SUBLANES + tm, FFN_CHUNK), F32),
                        pltpu.VMEM((tm, d), BF16), pltpu.VMEM((tm, d), F32)],
        compiler_params=pltpu.CompilerParams(
            dimension_semantics=("parallel", "arbitrary"), vmem_limit_bytes=VMEM_LIMIT),
        name="merge_ffn",
    )(stream, ya, ao, gates, wo, wm, fng, up, dww, dwb, down, fin)


def _row_tile(lp):
    for t in (ROW_TILE, 256, LANES):
        if lp % t == 0:
            return t
    raise ValueError(f"padded length {lp} is not a multiple of {LANES}")


def _split_bf16(w):
    hi = w.astype(BF16)
    return hi, (w - hi.astype(F32)).astype(BF16)


def kernel(x, meta_tokens, mix_norm_g, w_in, conv_ln_g, conv_ln_b, conv_dw_w, conv_dw_b, conv_pw_out,
           attn_w_o, w_merge_out, ffn_norm_g, ffn_up, ffn_dw_w, ffn_dw_b, ffn_down, final_norm_g):
    bsz, seq, d = x.shape
    depth = w_in.shape[0]
    seq_len = seq + N_META
    lp = -(-seq_len // LANES) * LANES
    lk = -(-lp // KEY_CHUNK) * KEY_CHUNK
    tm = _row_tile(lp)
    top_k = min(TOPK_MAX, seq_len // 4)
    assert top_k > N_META and conv_dw_w.shape[1] == CONV_WIDTH and ffn_dw_w.shape[1] == FFN_CONV_WIDTH
    d_conv = conv_pw_out.shape[1]
    d_ff = ffn_down.shape[1]
    assert d_ff % FFN_CHUNK == 0 and d % LANES == 0

    meta = jnp.broadcast_to(meta_tokens.astype(x.dtype)[None], (bsz, N_META, d))
    stream = jnp.concatenate([meta, x, jnp.zeros((bsz, lp - seq_len, d), x.dtype)], axis=1)
    tabs = _rope_tables(lp, HEAD_DIM, 1) + _rope_tables(lp, IDX_DIM, LANES // IDX_DIM)

    n_q, n_kv = N_HEADS * HEAD_DIM, N_KV_HEADS * HEAD_DIM
    n_qi = N_IDX_HEADS * IDX_DIM
    edges = [0, 2 * d_conv]
    for wdt in (n_q, n_kv, n_kv, n_qi, IDX_DIM, N_IDX_HEADS, 2 * d):
        edges.append(edges[-1] + wdt)
    row2 = lambda a: a.reshape(1, -1).astype(F32)
    pad_keys = lambda a: jnp.pad(a, ((0, 0), (0, lk - lp), (0, 0)))

    for l in range(depth):
        w = w_in[l]
        wu = w[:, edges[0]:edges[1]].astype(BF16)
        wq = (w[:, edges[1]:edges[2]] * (HEAD_DIM ** -0.5 * math.log2(math.e))).astype(BF16)
        wkv = w[:, edges[2]:edges[4]].astype(BF16)
        wqih, wqil = _split_bf16(w[:, edges[4]:edges[5]])
        wkwh, wkwl = _split_bf16(
            jnp.pad(w[:, edges[5]:edges[7]], ((0, 0), (0, LANES - IDX_DIM - N_IDX_HEADS))))
        wg = w[:, edges[7]:edges[8]].astype(BF16)

        glu, q, k, v, qs, kk, w_idx, gates = _in_proj(
            stream, row2(mix_norm_g[l]), wu, wq, wkv, wqih, wqil, wkwh, wkwl, wg, tabs, tm)
        ya = _conv_branch(glu, conv_dw_w[l], row2(conv_dw_b[l]), row2(conv_ln_g[l]), row2(conv_ln_b[l]),
                          conv_pw_out[l].astype(BF16), tm)
        v_t = jnp.swapaxes(pad_keys(v).reshape(bsz, lk, N_KV_HEADS, HEAD_DIM), 1, 3).swapaxes(1, 2)
        ao = _dsa(q, qs, jnp.swapaxes(w_idx, 1, 2), pad_keys(kk), pad_keys(k), v_t, seq_len, top_k)
        last = l == depth - 1
        stream = _merge_ffn(stream, ya, ao, gates, attn_w_o[l].astype(BF16), w_merge_out[l].astype(BF16),
                            row2(ffn_norm_g[l]), ffn_up[l].astype(BF16), ffn_dw_w[l], row2(ffn_dw_b[l]),
                            ffn_down[l].astype(BF16), row2(final_norm_g), tm)
        assert last, "the fused final norm assumes a single layer"
    return stream[:, N_META:seq_len]
```

```python
import functools
import math

import jax
import jax.numpy as jnp
from jax import lax
from jax.experimental import pallas as pl
from jax.experimental.pallas import tpu as pltpu

N_META = 16
CONV_WIDTH = 31
N_HEADS = 8
HEAD_DIM = 128
N_KV_HEADS = 2
N_IDX_HEADS = 8
IDX_DIM = 64
TOPK_MAX = 256
FFN_CONV_WIDTH = 3
ROPE_THETA = 10000.0
EPS = 1e-6

LANES = 128
SUBLANES = 8
Q_TILE = 128
KEY_CHUNK = 512
ROW_TILE = 384
FFN_ROW_TILE = 704
CONV_HALO = 32
CONV_ROWS = 64
FFN_CHUNK = 256
SPLIT_W = 4 * IDX_DIM
SAMPLE_STRIDE = 8
FALSI_FIRST_RUN = 11
FALSI_RUN = 2
MAX_FALSI_STEPS = 25
VMEM_LIMIT = 60 * 1024 * 1024

F32 = jnp.float32
BF16 = jnp.bfloat16
NEG_INF = float("-inf")
MASK_BIAS = -1e30
F32_LOWEST = float(jnp.finfo(jnp.float32).min)
NT_DIMS = (((1,), (1,)), ((), ()))


def _rope_tables(n_pos, dim, reps):
    half = dim // 2
    inv = ROPE_THETA ** (-jnp.arange(half, dtype=F32) / half)
    ang = jnp.arange(n_pos, dtype=jnp.int32).astype(F32)[:, None] * inv[None, :]
    cos, sin = jnp.cos(ang), jnp.sin(ang)
    cos_f = jnp.concatenate([cos, cos], axis=-1)
    sin_f = jnp.concatenate([-sin, sin], axis=-1)
    return jnp.tile(cos_f, (1, reps)), jnp.tile(sin_f, (1, reps))


def _rope128(x, cos, sin):
    return x * cos + pltpu.roll(x, HEAD_DIM // 2, axis=1) * sin


def _rope64(x, cos, sin, first_half):
    partner = jnp.where(first_half, pltpu.roll(x, LANES - IDX_DIM // 2, axis=1),
                        pltpu.roll(x, IDX_DIM // 2, axis=1))
    return x * cos + partner * sin


def _hi_lo(x):
    hi = x.astype(BF16).astype(F32)
    return hi, x - hi


def _conv_tile(glu, dww_ref, dwb_ref, buf_ref, conv_ref, interleave):
    tt, c = glu.shape

    @pl.when(pl.program_id(1) == 0)
    def _():
        buf_ref[0:CONV_HALO, :] = jnp.zeros((CONV_HALO, c), F32)

    @pl.when(pl.program_id(1) > 0)
    def _():
        buf_ref[0:CONV_HALO, :] = buf_ref[tt:tt + CONV_HALO, :]

    buf_ref[CONV_HALO:, :] = glu

    base = CONV_HALO - (CONV_WIDTH - 1)
    n_cb = c // LANES
    for cb in range(n_cb):
        cs = slice(cb * LANES, (cb + 1) * LANES)
        w_c = dww_ref[:, cs]
        b_c = dwb_ref[:, cs]
        for rb in range(tt // CONV_ROWS):
            r0 = rb * CONV_ROWS
            acc = jnp.broadcast_to(b_c, (CONV_ROWS, LANES))
            for ph in range(SUBLANES):
                rows = CONV_ROWS + (SUBLANES if ph else 0)
                part = None
                for a in range((base + CONV_WIDTH - 1) // SUBLANES + 1):
                    j = SUBLANES * a + ph - base
                    if 0 <= j < CONV_WIDTH:
                        term = buf_ref[r0 + SUBLANES * a:r0 + SUBLANES * a + rows, cs] * w_c[j:j + 1, :]
                        part = term if part is None else part + term
                acc = acc + part[ph:ph + CONV_ROWS, :]
            conv_ref[r0:r0 + CONV_ROWS, cs] = acc
        for piece in interleave[cb::n_cb]:
            piece()


def _in_proj_kernel(x_ref, g_ref, wu_ref, wq_ref, wkv_ref, wqih_ref, wqil_ref, wkwh_ref, wkwl_ref, wg_ref,
                    dww_ref, dwb_ref, lng_ref, lnb_ref, pw_ref, c128_ref, s128_ref, c64_ref, s64_ref,
                    ya_ref, q_ref, k_ref, v_ref, qs_ref, kk_ref, w_ref, gate_ref, buf_ref, conv_ref):
    d_conv = buf_ref.shape[-1]
    kv_w = k_ref.shape[-1]
    x = x_ref[...]
    h = x * lax.rsqrt(jnp.mean(x * x, axis=-1, keepdims=True) + EPS) * g_ref[...]
    hb = h.astype(BF16)
    hl = (h - hb.astype(F32)).astype(BF16)
    dot = functools.partial(jnp.dot, preferred_element_type=F32)

    c128, s128 = c128_ref[...], s128_ref[...]
    c64, s64 = c64_ref[...], s64_ref[...]
    lane = lax.broadcasted_iota(jnp.int32, (x.shape[0], LANES), 1)
    first_half = (lane & (IDX_DIM - 1)) < (IDX_DIM // 2)
    low_head = lane < IDX_DIM

    def gate_piece(c0, c1):
        def piece():
            gate_ref[:, c0:c1] = jax.nn.sigmoid(dot(hb, wg_ref[:, c0:c1])).astype(gate_ref.dtype)
        return piece

    def q_piece(h0, h1):
        def piece():
            q = dot(hb, wq_ref[:, h0 * HEAD_DIM:h1 * HEAD_DIM])
            for hd in range(h1 - h0):
                q_ref[:, (h0 + hd) * HEAD_DIM:(h0 + hd + 1) * HEAD_DIM] = _rope128(
                    q[:, hd * HEAD_DIM:(hd + 1) * HEAD_DIM], c128, s128).astype(q_ref.dtype)
        return piece

    def kv_piece():
        kv = dot(hb, wkv_ref[...])
        for hd in range(kv_w // HEAD_DIM):
            sl = slice(hd * HEAD_DIM, (hd + 1) * HEAD_DIM)
            k_ref[:, sl] = _rope128(kv[:, sl], c128, s128).astype(k_ref.dtype)
        v_ref[...] = kv[:, kv_w:].astype(v_ref.dtype)

    def split_dot(w_hi, w_lo):
        return dot(hb, w_hi) + dot(hl, w_hi) + dot(hb, w_lo)

    def qi_piece(b0, b1):
        def piece():
            qi = split_dot(wqih_ref[:, b0 * LANES:b1 * LANES], wqil_ref[:, b0 * LANES:b1 * LANES])
            for blk in range(b1 - b0):
                xr = _rope64(qi[:, blk * LANES:(blk + 1) * LANES], c64, s64, first_half)
                xs = pltpu.roll(xr, IDX_DIM, axis=1)
                for sub, dup in enumerate((jnp.where(low_head, xr, xs), jnp.where(low_head, xs, xr))):
                    hi, lo = _hi_lo(dup)
                    c0 = (2 * (b0 + blk) + sub) * SPLIT_W
                    qs_ref[:, c0:c0 + LANES] = hi.astype(BF16)
                    qs_ref[:, c0 + LANES:c0 + 2 * LANES] = lo.astype(BF16)
        return piece

    def kw_piece():
        kw = split_dot(wkwh_ref[...], wkwl_ref[...])
        kr = _rope64(kw, c64, s64, first_half)
        hi, lo = _hi_lo(jnp.where(low_head, kr, pltpu.roll(kr, IDX_DIM, axis=1)))
        mix = jnp.where(low_head, hi, lo).astype(BF16)
        kk_ref[:, 0:LANES] = mix
        kk_ref[:, LANES:2 * LANES] = mix
        w_ref[...] = kw[:, IDX_DIM:IDX_DIM + N_IDX_HEADS] * (N_IDX_HEADS ** -0.5 * IDX_DIM ** -0.5)

    n_gate, n_qh, n_qi = gate_ref.shape[-1], q_ref.shape[-1] // HEAD_DIM, wqih_ref.shape[-1] // LANES
    pieces = [gate_piece(0, n_gate // 4), gate_piece(n_gate // 4, n_gate // 2),
              gate_piece(n_gate // 2, 3 * n_gate // 4), gate_piece(3 * n_gate // 4, n_gate),
              q_piece(0, n_qh // 2), q_piece(n_qh // 2, n_qh), kv_piece,
              qi_piece(0, n_qi // 2), qi_piece(n_qi // 2, n_qi), kw_piece]

    u = dot(hb, wu_ref[...])
    _conv_tile(u[:, :d_conv] * jax.nn.sigmoid(u[:, d_conv:]), dww_ref, dwb_ref, buf_ref, conv_ref, pieces)
    hc = conv_ref[...]
    mu = jnp.mean(hc, axis=-1, keepdims=True)
    dlt = hc - mu
    var = jnp.mean(dlt * dlt, axis=-1, keepdims=True)
    hn = dlt * lax.rsqrt(var + EPS) * lng_ref[...] + lnb_ref[...]
    hs = hn * jax.nn.sigmoid(hn)
    ya_ref[...] = dot(hs.astype(BF16), pw_ref[...]).astype(ya_ref.dtype)


def _in_proj(stream, g, wu, wq, wkv, wqih, wqil, wkwh, wkwl, wg, dww, dwb, lng, lnb, pw, tabs, tm):
    bsz, lp, d = stream.shape
    nt = lp // tm
    d_conv = wu.shape[1] // 2
    row = lambda w: pl.BlockSpec((None, tm, w), lambda b, i: (b, i, 0))
    full = lambda a: pl.BlockSpec(a.shape, lambda b, i: (0,) * a.ndim, pipeline_mode=pl.Buffered(1))
    tab = pl.BlockSpec((tm, LANES), lambda b, i: (i, 0))
    outs = [(pw.shape[1], BF16), (wq.shape[1], BF16), (wkv.shape[1] // 2, BF16),
            (wkv.shape[1] // 2, BF16), (N_IDX_HEADS * SPLIT_W, BF16), (SPLIT_W, BF16),
            (N_IDX_HEADS, F32), (wg.shape[1], BF16)]
    weights = (g, wu, wq, wkv, wqih, wqil, wkwh, wkwl, wg, dww, dwb, lng, lnb, pw)
    return pl.pallas_call(
        _in_proj_kernel,
        grid=(bsz, nt),
        in_specs=[row(d)] + [full(a) for a in weights] + [tab, tab, tab, tab],
        out_specs=[row(w) for w, _ in outs],
        out_shape=[jax.ShapeDtypeStruct((bsz, lp, w), dt) for w, dt in outs],
        scratch_shapes=[pltpu.VMEM((CONV_HALO + tm, d_conv), F32), pltpu.VMEM((tm, d_conv), F32)],
        compiler_params=pltpu.CompilerParams(
            dimension_semantics=("parallel", "arbitrary"), vmem_limit_bytes=VMEM_LIMIT),
        name="in_proj_conv",
    )(stream, *weights, *tabs)


def _fold_rows(x, op):
    parts = [x[r * SUBLANES:(r + 1) * SUBLANES, :] for r in range(x.shape[0] // SUBLANES)]
    while len(parts) > 1:
        nxt = [op(parts[i], parts[i + 1]) for i in range(0, len(parts) - 1, 2)]
        parts = nxt + parts[len(parts) - len(parts) % 2:]
    return parts[0]


def _count(st_ref, thr, n_chunks, strict):
    tq = st_ref.shape[1]

    def body(c, acc):
        off = pl.multiple_of(c * KEY_CHUNK, KEY_CHUNK)
        blk = st_ref[pl.ds(off, KEY_CHUNK), :]
        hit = (blk > thr) if strict else (blk >= thr)
        return acc + _fold_rows(jnp.where(hit, 1.0, 0.0), jnp.add)

    acc = lax.fori_loop(0, n_chunks, body, jnp.zeros((SUBLANES, tq), F32))
    return jnp.sum(acc, axis=0, keepdims=True)


def _count_above(st_ref, thr, n_chunks):
    return _count(st_ref, thr, n_chunks, True)


def _count_at_least(st_ref, thr, n_chunks):
    return _count(st_ref, thr, n_chunks, False)


def _min_above(st_ref, lo, n_chunks):
    tq = st_ref.shape[1]

    def body(c, acc):
        off = pl.multiple_of(c * KEY_CHUNK, KEY_CHUNK)
        blk = st_ref[pl.ds(off, KEY_CHUNK), :]
        return jnp.minimum(acc, _fold_rows(jnp.where(blk > lo, blk, jnp.inf), jnp.minimum))

    acc = lax.fori_loop(0, n_chunks, body, jnp.full((SUBLANES, tq), jnp.inf, F32))
    return jnp.min(acc, axis=0, keepdims=True)


def _bracket_ends(st_ref, lo, hi, n_chunks):
    tq = st_ref.shape[1]

    def body(c, carry):
        bot, top = carry
        off = pl.multiple_of(c * KEY_CHUNK, KEY_CHUNK)
        blk = st_ref[pl.ds(off, KEY_CHUNK), :]
        bot = jnp.minimum(bot, _fold_rows(jnp.where(blk > lo, blk, jnp.inf), jnp.minimum))
        top = jnp.maximum(top, _fold_rows(jnp.where(blk <= hi, blk, NEG_INF), jnp.maximum))
        return bot, top

    bot, top = lax.fori_loop(0, n_chunks, body, (jnp.full((SUBLANES, tq), jnp.inf, F32),
                                                jnp.full((SUBLANES, tq), NEG_INF, F32)))
    return jnp.min(bot, axis=0, keepdims=True), jnp.max(top, axis=0, keepdims=True)


def _dsa_kernel(q_ref, qs_ref, wt_ref, kk_ref, k_ref, vt_ref, o_ref,
                st_ref, bias_ref, qsr_ref, qaug_ref, acc_ref, lg_ref, *, seq_len, top_k):
    tq = q_ref.shape[0]
    j = pl.program_id(1)
    q0 = j * tq
    n_chunks = (q0 + tq + KEY_CHUNK - 1) // KEY_CHUNK
    rep = N_HEADS // N_KV_HEADS
    k_sel = float(top_k)
    q_pos = q0 + lax.broadcasted_iota(jnp.int32, (1, tq), 1)

    for hd in range(N_IDX_HEADS):
        qsr_ref[hd * tq:(hd + 1) * tq, :] = qs_ref[:, hd * SPLIT_W:(hd + 1) * SPLIT_W]
    eye = (lax.broadcasted_iota(jnp.int32, (tq, tq), 0)
           == lax.broadcasted_iota(jnp.int32, (tq, tq), 1)).astype(BF16)
    for g in range(N_KV_HEADS):
        for r in range(rep):
            hd = g * rep + r
            qaug_ref[g, r * tq:(r + 1) * tq, 0:HEAD_DIM] = q_ref[:, hd * HEAD_DIM:(hd + 1) * HEAD_DIM]
            qaug_ref[g, r * tq:(r + 1) * tq, HEAD_DIM:] = eye

    w_t = wt_ref[...]

    def score_body(i, carry):
        mn, mx, s1, s2 = carry
        for u in range(2):
            c = jnp.minimum(2 * i + u, n_chunks - 1)
            off = pl.multiple_of(c * KEY_CHUNK, KEY_CHUNK)
            dots = lax.dot_general(kk_ref[pl.ds(off, KEY_CHUNK), :], qsr_ref[...], NT_DIMS,
                                   preferred_element_type=F32)
            sc = w_t[0:1, :] * jnp.maximum(dots[:, 0:tq], 0.0)
            for hd in range(1, N_IDX_HEADS):
                sc = sc + w_t[hd:hd + 1, :] * jnp.maximum(dots[:, hd * tq:(hd + 1) * tq], 0.0)
            st_ref[pl.ds(off, KEY_CHUNK), :] = sc
            mn = jnp.minimum(mn, _fold_rows(sc, jnp.minimum))
            mx = jnp.maximum(mx, _fold_rows(sc, jnp.maximum))
            fresh = jnp.where(2 * i + u < n_chunks, 1.0, 0.0)
            for r in range(0, KEY_CHUNK, SAMPLE_STRIDE * SUBLANES):
                smp = sc[r:r + SUBLANES, :]
                s1 = s1 + fresh * smp
                s2 = s2 + fresh * (smp * smp)
        return mn, mx, s1, s2

    zeros8 = jnp.zeros((SUBLANES, tq), F32)
    mn, mx, s1, s2 = lax.fori_loop(
        0, (n_chunks + 1) // 2, score_body,
        (jnp.full((SUBLANES, tq), jnp.inf, F32), jnp.full((SUBLANES, tq), NEG_INF, F32), zeros8, zeros8))
    row_min = jnp.min(mn, axis=0, keepdims=True)
    row_max = jnp.max(mx, axis=0, keepdims=True)
    n_smp = (n_chunks * (KEY_CHUNK // SAMPLE_STRIDE)).astype(F32)
    smp_mean = jnp.sum(s1, axis=0, keepdims=True) / n_smp
    smp_var = jnp.maximum(jnp.sum(s2, axis=0, keepdims=True) / n_smp - smp_mean * smp_mean, 0.0)

    meta_pos = lax.broadcasted_iota(jnp.int32, (N_META, tq), 0)
    st_ref[0:N_META, :] = jnp.where(meta_pos <= q_pos, jnp.inf, NEG_INF)

    def causal_body(blk, carry):
        off = pl.multiple_of(blk * tq, tq)
        key_pos = off + lax.broadcasted_iota(jnp.int32, (tq, tq), 0)
        keep = (key_pos <= q_pos) & (key_pos < seq_len)
        st_ref[pl.ds(off, tq), :] = jnp.where(keep, st_ref[pl.ds(off, tq), :], NEG_INF)
        return carry

    lax.fori_loop(j, n_chunks * (KEY_CHUNK // tq), causal_body, 0)

    select_all = ((q_pos + 1) <= top_k) | (q_pos >= seq_len)

    def search():
        lo0 = jnp.where(select_all, 0.0, jnp.minimum(row_min, 0.0) * 2.0 - 1.0)
        hi0 = jnp.where(select_all, 0.0, row_max)

        def bisect(_, carry):
            lo, hi = carry
            mid = 0.5 * lo + 0.5 * hi
            ge = _count_above(st_ref, mid, n_chunks) >= k_sel
            return jnp.where(ge, mid, lo), jnp.where(ge, hi, mid)

        def snap(lo, hi):
            cand = _min_above(st_ref, lo, n_chunks)
            done = select_all | (_count_above(st_ref, cand, n_chunks) < k_sel)
            return jnp.where(done, lo, cand), hi, cand, jnp.sum(jnp.where(done, 0.0, 1.0))

        def more(state):
            lo, hi = lax.fori_loop(0, 4, bisect, state[:2])
            return snap(lo, hi)

        n_vis = jnp.minimum(q_pos + 1, seq_len).astype(F32)
        n_meta = jnp.minimum(q_pos + 1, N_META).astype(F32)
        tail = jnp.clip((k_sel - n_meta - 0.5) / jnp.maximum(n_vis - n_meta, 1.0), 1e-6, 1.0 - 1e-6)
        tt = jnp.sqrt(-2.0 * jnp.log(jnp.minimum(tail, 1.0 - tail)))
        z = tt - ((0.010328 * tt + 0.802853) * tt + 2.515517) / (
            ((0.001308 * tt + 0.189269) * tt + 1.432788) * tt + 1.0)
        guess = smp_mean + jnp.where(tail < 0.5, z, -z) * jnp.sqrt(smp_var)
        target = k_sel - 0.5

        def wide(c_lo, c_hi):
            return jnp.sum(jnp.where(jnp.logical_not(select_all) & (c_lo - c_hi > 2.0), 1.0, 0.0))

        def falsi(state):
            lo, hi, c_lo, c_hi, f_lo, f_hi, side, it = state
            t = jnp.where(it == 0, guess, lo + (hi - lo) * (f_lo / (f_lo - f_hi)))
            t = jnp.where((t > lo) & (t < hi), t, 0.5 * lo + 0.5 * hi)
            c = _count_above(st_ref, t, n_chunks)
            ge = c >= k_sel
            f = c - target
            f_lo = jnp.where(ge, f, jnp.where(side < 0.0, 0.5 * f_lo, f_lo))
            f_hi = jnp.where(ge, jnp.where(side > 0.0, 0.5 * f_hi, f_hi), f)
            lo, c_lo = jnp.where(ge, t, lo), jnp.where(ge, c, c_lo)
            hi, c_hi = jnp.where(ge, hi, t), jnp.where(ge, c_hi, c)
            return lo, hi, c_lo, c_hi, f_lo, f_hi, jnp.where(ge, 1.0, -1.0), it + 1

        def falsi_run(steps, state):
            state = lax.fori_loop(0, steps, lambda _, s: falsi(s), state)
            return state, wide(state[2], state[3])

        c_lo0, c_hi0 = n_vis, n_meta
        state = (lo0, hi0, c_lo0, c_hi0, c_lo0 - target, c_hi0 - target, jnp.zeros((1, tq), F32),
                 jnp.int32(0))
        state, n_wide = lax.while_loop(
            lambda s: (s[0][7] < MAX_FALSI_STEPS) & (s[1] > 0.0),
            lambda s: falsi_run(FALSI_RUN, s[0]), falsi_run(FALSI_FIRST_RUN, state))
        lo, hi, c_lo, c_hi = state[:4]
        bot, top = _bracket_ends(st_ref, lo, hi, n_chunks)
        is_top = (k_sel - c_hi) <= 1.0
        single = bot == top
        kth2 = jnp.where(is_top, top, bot)
        n_gt2 = jnp.where(single | is_top, c_hi, c_hi + 1.0)
        n_ge2 = jnp.where(single | jnp.logical_not(is_top), c_lo, c_hi + 1.0)

        def robust():
            narrow = select_all | (c_lo - c_hi <= 1.0)
            done = narrow | (_count_above(st_ref, bot, n_chunks) < k_sel)
            first = (jnp.where(done, lo, bot), hi, bot, jnp.sum(jnp.where(done, 0.0, 1.0)))
            kth = lax.while_loop(lambda s: s[3] > 0.0, more, first)[2]
            return kth, _count_above(st_ref, kth, n_chunks), _count_at_least(st_ref, kth, n_chunks)

        kth, n_gt, n_ge = lax.cond(n_wide > 0.0, robust, lambda: (kth2, n_gt2, n_ge2))

        thr = jnp.where(select_all, F32_LOWEST, kth)
        eq_take = jnp.where(select_all | (n_ge <= k_sel), jnp.inf, k_sel - n_gt)
        return thr, eq_take

    def keep_all():
        return jnp.full((1, tq), F32_LOWEST, F32), jnp.full((1, tq), jnp.inf, F32)

    thr, eq_take = lax.cond(q0 + tq > top_k, search, keep_all)
    any_tie = jnp.max(jnp.where(eq_take < jnp.inf, 1.0, 0.0)) > 0.0

    @pl.when(jnp.logical_not(any_tie))
    def _():
        def bias_body(c, carry):
            off = pl.multiple_of(c * KEY_CHUNK, KEY_CHUNK)
            sel = st_ref[pl.ds(off, KEY_CHUNK), :] >= thr
            bias_ref[pl.ds(off, KEY_CHUNK), :] = jnp.where(sel, 0.0, MASK_BIAS).astype(BF16)
            return carry

        lax.fori_loop(0, n_chunks, bias_body, 0)

    @pl.when(any_tie)
    def _():
        tri = (lax.broadcasted_iota(jnp.int32, (KEY_CHUNK, KEY_CHUNK), 1)
               <= lax.broadcasted_iota(jnp.int32, (KEY_CHUNK, KEY_CHUNK), 0)).astype(BF16)

        def tie_body(c, seen):
            off = pl.multiple_of(c * KEY_CHUNK, KEY_CHUNK)
            sc = st_ref[pl.ds(off, KEY_CHUNK), :]
            eq = jnp.where(sc == thr, 1.0, 0.0)
            rank = seen + jnp.dot(tri, eq.astype(BF16), preferred_element_type=F32)
            sel = (sc > thr) | ((sc == thr) & (rank <= eq_take))
            bias_ref[pl.ds(off, KEY_CHUNK), :] = jnp.where(sel, 0.0, MASK_BIAS).astype(BF16)
            return seen + jnp.sum(eq, axis=0, keepdims=True)

        lax.fori_loop(0, n_chunks, tie_body, jnp.zeros((1, tq), F32))

    acc_ref[...] = jnp.zeros(acc_ref.shape, F32)

    def qk(c, slot):
        off = pl.multiple_of(c * KEY_CHUNK, KEY_CHUNK)
        bias = bias_ref[pl.ds(off, KEY_CHUNK), :]
        for g in range(N_KV_HEADS):
            k_g = k_ref[pl.ds(off, KEY_CHUNK), g * HEAD_DIM:(g + 1) * HEAD_DIM]
            lg_ref[slot, g] = lax.dot_general(jnp.concatenate([k_g, bias], axis=1), qaug_ref[g], NT_DIMS,
                                              preferred_element_type=F32)

    def softmax_pv(c, slot, stats):
        off = pl.multiple_of(c * KEY_CHUNK, KEY_CHUNK)
        out = []
        for g in range(N_KV_HEADS):
            m_old, l_old = stats[2 * g], stats[2 * g + 1]
            logits = lg_ref[slot, g]
            m_new = jnp.maximum(m_old, jnp.max(logits, axis=0, keepdims=True))
            p = jnp.exp2(logits - m_new)
            alpha = jnp.exp2(m_old - m_new)
            l_new = alpha * l_old + jnp.sum(p, axis=0, keepdims=True)
            pv = jnp.dot(vt_ref[g, :, pl.ds(off, KEY_CHUNK)], p.astype(BF16), preferred_element_type=F32)
            acc_ref[g] = alpha * acc_ref[g] + pv
            out += [m_new, l_new]
        return tuple(out)

    last = n_chunks - 1
    qk(0, 0)

    def attn_body(i, stats):
        c = 2 * i
        qk(jnp.minimum(c + 1, last), 1)
        stats = softmax_pv(c, 0, stats)

        def second_half(stats):
            qk(jnp.minimum(c + 2, last), 0)
            return softmax_pv(c + 1, 1, stats)

        return lax.cond(c + 1 < n_chunks, second_half, lambda s: s, stats)

    init = (jnp.full((1, rep * tq), NEG_INF, F32), jnp.zeros((1, rep * tq), F32)) * N_KV_HEADS
    stats = lax.fori_loop(0, (n_chunks + 1) // 2, attn_body, init)

    for g in range(N_KV_HEADS):
        o_t = acc_ref[g] / stats[2 * g + 1]
        for r in range(rep):
            hd = g * rep + r
            o_ref[:, hd * HEAD_DIM:(hd + 1) * HEAD_DIM] = o_t[:, r * tq:(r + 1) * tq].T.astype(o_ref.dtype)


def _dsa(q, qs, w_t, kk, k, v_t, seq_len, top_k):
    bsz, lp, dq = q.shape
    lk = kk.shape[1]
    rep = N_HEADS // N_KV_HEADS
    qrow = lambda w: pl.BlockSpec((None, Q_TILE, w), lambda b, j: (b, j, 0))
    keys = lambda w: pl.BlockSpec((None, lk, w), lambda b, j: (b, 0, 0))
    kern = functools.partial(_dsa_kernel, seq_len=seq_len, top_k=top_k)
    return pl.pallas_call(
        kern,
        grid=(bsz, lp // Q_TILE),
        in_specs=[qrow(dq), qrow(qs.shape[-1]),
                  pl.BlockSpec((None, N_IDX_HEADS, Q_TILE), lambda b, j: (b, 0, j)),
                  keys(SPLIT_W), keys(k.shape[-1]),
                  pl.BlockSpec((None, N_KV_HEADS, HEAD_DIM, lk), lambda b, j: (b, 0, 0, 0))],
        out_specs=qrow(dq),
        out_shape=jax.ShapeDtypeStruct((bsz, lp, dq), BF16),
        scratch_shapes=[pltpu.VMEM((lk, Q_TILE), F32), pltpu.VMEM((lk, Q_TILE), BF16),
                        pltpu.VMEM((N_IDX_HEADS * Q_TILE, SPLIT_W), BF16),
                        pltpu.VMEM((N_KV_HEADS, rep * Q_TILE, 2 * HEAD_DIM), BF16),
                        pltpu.VMEM((N_KV_HEADS, HEAD_DIM, rep * Q_TILE), F32),
                        pltpu.VMEM((2, N_KV_HEADS, KEY_CHUNK, rep * Q_TILE), F32)],
        compiler_params=pltpu.CompilerParams(
            dimension_semantics=("parallel", "arbitrary"), vmem_limit_bytes=VMEM_LIMIT),
        name="dsa_attention",
    )(q, qs, w_t, kk, k, v_t)


def _merge_ffn_kernel(s_ref, ya_ref, ao_ref, gate_ref, wo_ref, wm_ref, fng_ref, up_ref, dww_ref,
                      dwb_ref, down_ref, fin_ref, out_ref, carry_ref, ubuf_ref, hb_ref, acc_ref,
                      obuf_ref, out_sem, *, seq_len, n_b, n_t):
    tm, d = s_ref.shape
    d_ff = down_ref.shape[0]
    b, i = pl.program_id(0), pl.program_id(1)

    @pl.when(i == 0)
    def _():
        carry_ref[...] = jnp.zeros(carry_ref.shape, F32)

    y_b = jnp.dot(ao_ref[...], wo_ref[...], preferred_element_type=F32)
    mix = gate_ref[:, :d].astype(F32) * ya_ref[...].astype(F32) + gate_ref[:, d:].astype(F32) * y_b
    stream = s_ref[...] + jnp.dot(mix.astype(BF16), wm_ref[...], preferred_element_type=F32)
    h = stream * lax.rsqrt(jnp.mean(stream * stream, axis=-1, keepdims=True) + EPS) * fng_ref[...]
    hb_ref[...] = h.astype(BF16)
    acc_ref[...] = stream

    def conv3(u, col, slot):
        ubuf_ref[slot, 0:SUBLANES, :] = carry_ref[:, pl.ds(col, FFN_CHUNK)]
        ubuf_ref[slot, SUBLANES:, :] = u
        carry_ref[:, pl.ds(col, FFN_CHUNK)] = u[tm - SUBLANES:, :]
        w = dww_ref[:, pl.ds(col, FFN_CHUNK)]
        out = dwb_ref[:, pl.ds(col, FFN_CHUNK)] + w[2:3, :] * u
        out = out + w[1:2, :] * ubuf_ref[slot, SUBLANES - 1:SUBLANES - 1 + tm, :]
        return out + w[0:1, :] * ubuf_ref[slot, SUBLANES - 2:SUBLANES - 2 + tm, :]

    def chunk(c, carry):
        col_a = pl.multiple_of(c * FFN_CHUNK, FFN_CHUNK)
        col_b = pl.multiple_of(d_ff + c * FFN_CHUNK, FFN_CHUNK)
        hb = hb_ref[...]
        a = conv3(jnp.dot(hb, up_ref[:, pl.ds(col_a, FFN_CHUNK)], preferred_element_type=F32), col_a, 0)
        b = conv3(jnp.dot(hb, up_ref[:, pl.ds(col_b, FFN_CHUNK)], preferred_element_type=F32), col_b, 1)
        act = (a * jax.nn.sigmoid(a) * b).astype(BF16)
        acc_ref[...] += jnp.dot(act, down_ref[pl.ds(col_a, FFN_CHUNK), :], preferred_element_type=F32)
        return carry

    lax.fori_loop(0, d_ff // FFN_CHUNK, chunk, 0, unroll=True)

    step = b * n_t + i
    slot = step % 2
    y = acc_ref[...]
    obuf_ref[slot] = y * lax.rsqrt(jnp.mean(y * y, axis=-1, keepdims=True) + EPS) * fin_ref[...]

    def tile_copy(kind, bb, tile, buf):
        if kind == "first":
            src0, rows, dst0 = N_META, min(tm, seq_len) - N_META, 0
        elif kind == "last":
            src0, rows, dst0 = 0, seq_len - (n_t - 1) * tm, (n_t - 1) * tm - N_META
        else:
            src0, rows, dst0 = 0, tm, pl.multiple_of(tile * tm - N_META, SUBLANES)
        return pltpu.make_async_copy(obuf_ref.at[buf, pl.ds(src0, rows)],
                                     out_ref.at[bb, pl.ds(dst0, rows)], out_sem.at[buf])

    def per_kind(tile, fn):
        pl.when(tile == 0)(lambda: fn("first"))
        if n_t > 2:
            pl.when((tile > 0) & (tile < n_t - 1))(lambda: fn("middle"))
        if n_t > 1:
            pl.when(tile == n_t - 1)(lambda: fn("last"))

    per_kind(i, lambda kind: tile_copy(kind, b, i, slot).start())

    @pl.when(step > 0)
    def _():
        prev_tile = jnp.where(i == 0, n_t - 1, i - 1)
        prev_b = jnp.where(i == 0, b - 1, b)
        per_kind(prev_tile, lambda kind: tile_copy(kind, prev_b, prev_tile, 1 - slot).wait())

    @pl.when(step == n_b * n_t - 1)
    def _():
        per_kind(i, lambda kind: tile_copy(kind, b, i, slot).wait())


def _merge_ffn(stream, ya, ao, gates, wo, wm, fng, up, dww, dwb, down, fin, tm, seq_len):
    bsz, lp, d = stream.shape
    row = lambda w: pl.BlockSpec((None, tm, w), lambda b, i: (b, i, 0))
    full = lambda a: pl.BlockSpec(a.shape, lambda b, i: (0,) * a.ndim, pipeline_mode=pl.Buffered(1))
    return pl.pallas_call(
        functools.partial(_merge_ffn_kernel, seq_len=seq_len, n_b=bsz, n_t=lp // tm),
        grid=(bsz, lp // tm),
        in_specs=[row(d), row(d), row(d), row(2 * d), full(wo), full(wm), full(fng), full(up),
                  full(dww), full(dwb), full(down), full(fin)],
        out_specs=pl.BlockSpec(memory_space=pl.ANY),
        out_shape=jax.ShapeDtypeStruct((bsz, seq_len - N_META, d), F32),
        scratch_shapes=[pltpu.VMEM((SUBLANES, up.shape[1]), F32),
                        pltpu.VMEM((2, SUBLANES + tm, FFN_CHUNK), F32),
                        pltpu.VMEM((tm, d), BF16), pltpu.VMEM((tm, d), F32),
                        pltpu.VMEM((2, tm, d), F32), pltpu.SemaphoreType.DMA((2,))],
        compiler_params=pltpu.CompilerParams(
            dimension_semantics=("arbitrary", "arbitrary"), vmem_limit_bytes=VMEM_LIMIT),
        name="merge_ffn",
    )(stream, ya, ao, gates, wo, wm, fng, up, dww, dwb, down, fin)


def _row_tile(lp, prefer=ROW_TILE):
    for t in (prefer, ROW_TILE, 256, LANES):
        if lp % t == 0:
            return t
    raise ValueError(f"padded length {lp} is not a multiple of {LANES}")


def _split_bf16(w):
    hi = w.astype(BF16)
    return hi, (w - hi.astype(F32)).astype(BF16)


def kernel(x, meta_tokens, mix_norm_g, w_in, conv_ln_g, conv_ln_b, conv_dw_w, conv_dw_b, conv_pw_out,
           attn_w_o, w_merge_out, ffn_norm_g, ffn_up, ffn_dw_w, ffn_dw_b, ffn_down, final_norm_g):
    bsz, seq, d = x.shape
    depth = w_in.shape[0]
    seq_len = seq + N_META
    lp = -(-seq_len // LANES) * LANES
    lk = -(-lp // KEY_CHUNK) * KEY_CHUNK
    tm = _row_tile(lp)
    top_k = min(TOPK_MAX, seq_len // 4)
    assert top_k > N_META and conv_dw_w.shape[1] == CONV_WIDTH and ffn_dw_w.shape[1] == FFN_CONV_WIDTH
    d_conv = conv_pw_out.shape[1]
    d_ff = ffn_down.shape[1]
    assert d_ff % FFN_CHUNK == 0 and d % LANES == 0

    meta = jnp.broadcast_to(meta_tokens.astype(x.dtype)[None], (bsz, N_META, d))
    stream = jnp.concatenate([meta, x, jnp.zeros((bsz, lp - seq_len, d), x.dtype)], axis=1)
    tabs = _rope_tables(lp, HEAD_DIM, 1) + _rope_tables(lp, IDX_DIM, LANES // IDX_DIM)

    n_q, n_kv = N_HEADS * HEAD_DIM, N_KV_HEADS * HEAD_DIM
    n_qi = N_IDX_HEADS * IDX_DIM
    edges = [0, 2 * d_conv]
    for wdt in (n_q, n_kv, n_kv, n_qi, IDX_DIM, N_IDX_HEADS, 2 * d):
        edges.append(edges[-1] + wdt)
    row2 = lambda a: a.reshape(1, -1).astype(F32)
    pad_keys = lambda a: jnp.pad(a, ((0, 0), (0, lk - lp), (0, 0)))

    for l in range(depth):
        w = w_in[l]
        wu = w[:, edges[0]:edges[1]].astype(BF16)
        wq = (w[:, edges[1]:edges[2]] * (HEAD_DIM ** -0.5 * math.log2(math.e))).astype(BF16)
        wkv = w[:, edges[2]:edges[4]].astype(BF16)
        wqih, wqil = _split_bf16(w[:, edges[4]:edges[5]])
        wkwh, wkwl = _split_bf16(
            jnp.pad(w[:, edges[5]:edges[7]], ((0, 0), (0, LANES - IDX_DIM - N_IDX_HEADS))))
        wg = w[:, edges[7]:edges[8]].astype(BF16)

        ya, q, k, v, qs, kk, w_idx, gates = _in_proj(
            stream, row2(mix_norm_g[l]), wu, wq, wkv, wqih, wqil, wkwh, wkwl, wg, conv_dw_w[l],
            row2(conv_dw_b[l]), row2(conv_ln_g[l]), row2(conv_ln_b[l]), conv_pw_out[l].astype(BF16), tabs, tm)
        v_t = jnp.swapaxes(pad_keys(v).reshape(bsz, lk, N_KV_HEADS, HEAD_DIM), 1, 3).swapaxes(1, 2)
        ao = _dsa(q, qs, jnp.swapaxes(w_idx, 1, 2), pad_keys(kk), pad_keys(k), v_t, seq_len, top_k)
        assert l == depth - 1, "merge_ffn applies the final norm and drops the meta rows: single layer only"
        out = _merge_ffn(stream, ya, ao, gates, attn_w_o[l].astype(BF16), w_merge_out[l].astype(BF16),
                         row2(ffn_norm_g[l]), ffn_up[l].astype(BF16), ffn_dw_w[l], row2(ffn_dw_b[l]),
                         ffn_down[l].astype(BF16), row2(final_norm_g), _row_tile(lp, FFN_ROW_TILE), seq_len)
    return out
```

```python
import functools
import math

import jax
import jax.numpy as jnp
from jax import lax
from jax.experimental import pallas as pl
from jax.experimental.pallas import tpu as pltpu

N_META = 16
CONV_WIDTH = 31
N_HEADS = 8
HEAD_DIM = 128
N_KV_HEADS = 2
N_IDX_HEADS = 8
IDX_DIM = 64
TOPK_MAX = 256
FFN_CONV_WIDTH = 3
ROPE_THETA = 10000.0
EPS = 1e-6

LANES = 128
SUBLANES = 8
Q_TILE = 128
KEY_CHUNK = 512
ROW_TILE = 384
FFN_ROW_TILE = 704
CONV_HALO = 32
CONV_ROWS = 64
FFN_CHUNK = 256
IDX_W = LANES
SAMPLE_STRIDE = 8
FALSI_FIRST_RUN = 11
FALSI_RUN = 2
MAX_FALSI_STEPS = 25
VMEM_LIMIT = 60 * 1024 * 1024

F32 = jnp.float32
BF16 = jnp.bfloat16
NEG_INF = float("-inf")
MASK_BIAS = -1e30
F32_LOWEST = float(jnp.finfo(jnp.float32).min)
NT_DIMS = (((1,), (1,)), ((), ()))


def _rope_tables(n_pos, dim, reps):
    half = dim // 2
    inv = ROPE_THETA ** (-jnp.arange(half, dtype=F32) / half)
    ang = jnp.arange(n_pos, dtype=jnp.int32).astype(F32)[:, None] * inv[None, :]
    cos, sin = jnp.cos(ang), jnp.sin(ang)
    cos_f = jnp.concatenate([cos, cos], axis=-1)
    sin_f = jnp.concatenate([-sin, sin], axis=-1)
    return jnp.tile(cos_f, (1, reps)), jnp.tile(sin_f, (1, reps))


def _rope128(x, cos, sin):
    return x * cos + pltpu.roll(x, HEAD_DIM // 2, axis=1) * sin


def _rope64(x, cos, sin, first_half):
    partner = jnp.where(first_half, pltpu.roll(x, LANES - IDX_DIM // 2, axis=1),
                        pltpu.roll(x, IDX_DIM // 2, axis=1))
    return x * cos + partner * sin


def _conv_tile(glu, dww_ref, dwb_ref, buf_ref, conv_ref, interleave):
    tt, c = glu.shape

    @pl.when(pl.program_id(1) == 0)
    def _():
        buf_ref[0:CONV_HALO, :] = jnp.zeros((CONV_HALO, c), F32)

    @pl.when(pl.program_id(1) > 0)
    def _():
        buf_ref[0:CONV_HALO, :] = buf_ref[tt:tt + CONV_HALO, :]

    buf_ref[CONV_HALO:, :] = glu

    base = CONV_HALO - (CONV_WIDTH - 1)
    n_cb = c // LANES
    for cb in range(n_cb):
        cs = slice(cb * LANES, (cb + 1) * LANES)
        w_c = dww_ref[:, cs]
        b_c = dwb_ref[:, cs]
        for rb in range(tt // CONV_ROWS):
            r0 = rb * CONV_ROWS
            acc = jnp.broadcast_to(b_c, (CONV_ROWS, LANES))
            for ph in range(SUBLANES):
                rows = CONV_ROWS + (SUBLANES if ph else 0)
                part = None
                for a in range((base + CONV_WIDTH - 1) // SUBLANES + 1):
                    j = SUBLANES * a + ph - base
                    if 0 <= j < CONV_WIDTH:
                        term = buf_ref[r0 + SUBLANES * a:r0 + SUBLANES * a + rows, cs] * w_c[j:j + 1, :]
                        part = term if part is None else part + term
                acc = acc + part[ph:ph + CONV_ROWS, :]
            conv_ref[r0:r0 + CONV_ROWS, cs] = acc
        for piece in interleave[cb::n_cb]:
            piece()


def _in_proj_kernel(x_ref, g_ref, wu_ref, wq_ref, wkv_ref, wqi_ref, wkw_ref, wg_ref,
                    dww_ref, dwb_ref, lng_ref, lnb_ref, pw_ref, c128_ref, s128_ref, c64_ref, s64_ref,
                    ya_ref, q_ref, k_ref, v_ref, qs_ref, kk_ref, w_ref, gate_ref, buf_ref, conv_ref):
    d_conv = buf_ref.shape[-1]
    kv_w = k_ref.shape[-1]
    x = x_ref[...]
    h = x * lax.rsqrt(jnp.mean(x * x, axis=-1, keepdims=True) + EPS) * g_ref[...]
    hb = h.astype(BF16)
    dot = functools.partial(jnp.dot, preferred_element_type=F32)

    c128, s128 = c128_ref[...], s128_ref[...]
    c64, s64 = c64_ref[...], s64_ref[...]
    lane = lax.broadcasted_iota(jnp.int32, (x.shape[0], LANES), 1)
    first_half = (lane & (IDX_DIM - 1)) < (IDX_DIM // 2)
    low_head = lane < IDX_DIM

    def gate_piece(c0, c1):
        def piece():
            gate_ref[:, c0:c1] = jax.nn.sigmoid(dot(hb, wg_ref[:, c0:c1])).astype(gate_ref.dtype)
        return piece

    def q_piece(h0, h1):
        def piece():
            q = dot(hb, wq_ref[:, h0 * HEAD_DIM:h1 * HEAD_DIM])
            for hd in range(h1 - h0):
                q_ref[:, (h0 + hd) * HEAD_DIM:(h0 + hd + 1) * HEAD_DIM] = _rope128(
                    q[:, hd * HEAD_DIM:(hd + 1) * HEAD_DIM], c128, s128).astype(q_ref.dtype)
        return piece

    def kv_piece():
        kv = dot(hb, wkv_ref[...])
        for hd in range(kv_w // HEAD_DIM):
            sl = slice(hd * HEAD_DIM, (hd + 1) * HEAD_DIM)
            k_ref[:, sl] = _rope128(kv[:, sl], c128, s128).astype(k_ref.dtype)
        v_ref[...] = kv[:, kv_w:].astype(v_ref.dtype)

    def qi_piece(b0, b1):
        def piece():
            qi = dot(hb, wqi_ref[:, b0 * LANES:b1 * LANES])
            for blk in range(b1 - b0):
                xr = _rope64(qi[:, blk * LANES:(blk + 1) * LANES], c64, s64, first_half)
                xs = pltpu.roll(xr, IDX_DIM, axis=1)
                for sub, head in enumerate((xr, xs)):
                    c0 = (2 * (b0 + blk) + sub) * IDX_W
                    qs_ref[:, c0:c0 + IDX_W] = jnp.where(low_head, head, 0.0).astype(BF16)
        return piece

    def kw_piece():
        kw = dot(hb, wkw_ref[...])
        kk_ref[...] = jnp.where(low_head, _rope64(kw, c64, s64, first_half), 0.0).astype(BF16)
        w_ref[...] = kw[:, IDX_DIM:IDX_DIM + N_IDX_HEADS] * (N_IDX_HEADS ** -0.5 * IDX_DIM ** -0.5)

    n_gate, n_qh, n_qi = gate_ref.shape[-1], q_ref.shape[-1] // HEAD_DIM, wqi_ref.shape[-1] // LANES
    pieces = [gate_piece(0, n_gate // 4), gate_piece(n_gate // 4, n_gate // 2),
              gate_piece(n_gate // 2, 3 * n_gate // 4), gate_piece(3 * n_gate // 4, n_gate),
              q_piece(0, n_qh // 2), q_piece(n_qh // 2, n_qh), kv_piece,
              qi_piece(0, n_qi // 2), qi_piece(n_qi // 2, n_qi), kw_piece]

    u = dot(hb, wu_ref[...])
    _conv_tile(u[:, :d_conv] * jax.nn.sigmoid(u[:, d_conv:]), dww_ref, dwb_ref, buf_ref, conv_ref, pieces)
    hc = conv_ref[...]
    mu = jnp.mean(hc, axis=-1, keepdims=True)
    dlt = hc - mu
    var = jnp.mean(dlt * dlt, axis=-1, keepdims=True)
    hn = dlt * lax.rsqrt(var + EPS) * lng_ref[...] + lnb_ref[...]
    hs = hn * jax.nn.sigmoid(hn)
    ya_ref[...] = dot(hs.astype(BF16), pw_ref[...]).astype(ya_ref.dtype)


def _in_proj(stream, g, wu, wq, wkv, wqi, wkw, wg, dww, dwb, lng, lnb, pw, tabs, tm):
    bsz, lp, d = stream.shape
    nt = lp // tm
    d_conv = wu.shape[1] // 2
    row = lambda w: pl.BlockSpec((None, tm, w), lambda b, i: (b, i, 0))
    full = lambda a: pl.BlockSpec(a.shape, lambda b, i: (0,) * a.ndim, pipeline_mode=pl.Buffered(1))
    tab = pl.BlockSpec((tm, LANES), lambda b, i: (i, 0))
    outs = [(pw.shape[1], BF16), (wq.shape[1], BF16), (wkv.shape[1] // 2, BF16),
            (wkv.shape[1] // 2, BF16), (N_IDX_HEADS * IDX_W, BF16), (IDX_W, BF16),
            (N_IDX_HEADS, F32), (wg.shape[1], BF16)]
    weights = (g, wu, wq, wkv, wqi, wkw, wg, dww, dwb, lng, lnb, pw)
    return pl.pallas_call(
        _in_proj_kernel,
        grid=(bsz, nt),
        in_specs=[row(d)] + [full(a) for a in weights] + [tab, tab, tab, tab],
        out_specs=[row(w) for w, _ in outs],
        out_shape=[jax.ShapeDtypeStruct((bsz, lp, w), dt) for w, dt in outs],
        scratch_shapes=[pltpu.VMEM((CONV_HALO + tm, d_conv), F32), pltpu.VMEM((tm, d_conv), F32)],
        compiler_params=pltpu.CompilerParams(
            dimension_semantics=("parallel", "arbitrary"), vmem_limit_bytes=VMEM_LIMIT),
        name="in_proj_conv",
    )(stream, *weights, *tabs)


def _fold_rows(x, op):
    parts = [x[r * SUBLANES:(r + 1) * SUBLANES, :] for r in range(x.shape[0] // SUBLANES)]
    while len(parts) > 1:
        nxt = [op(parts[i], parts[i + 1]) for i in range(0, len(parts) - 1, 2)]
        parts = nxt + parts[len(parts) - len(parts) % 2:]
    return parts[0]


def _count(st_ref, thr, n_chunks, strict):
    tq = st_ref.shape[1]

    def body(c, acc):
        off = pl.multiple_of(c * KEY_CHUNK, KEY_CHUNK)
        blk = st_ref[pl.ds(off, KEY_CHUNK), :]
        hit = (blk > thr) if strict else (blk >= thr)
        return acc + _fold_rows(jnp.where(hit, 1.0, 0.0), jnp.add)

    acc = lax.fori_loop(0, n_chunks, body, jnp.zeros((SUBLANES, tq), F32))
    return jnp.sum(acc, axis=0, keepdims=True)


def _count_above(st_ref, thr, n_chunks):
    return _count(st_ref, thr, n_chunks, True)


def _count_at_least(st_ref, thr, n_chunks):
    return _count(st_ref, thr, n_chunks, False)


def _min_above(st_ref, lo, n_chunks):
    tq = st_ref.shape[1]

    def body(c, acc):
        off = pl.multiple_of(c * KEY_CHUNK, KEY_CHUNK)
        blk = st_ref[pl.ds(off, KEY_CHUNK), :]
        return jnp.minimum(acc, _fold_rows(jnp.where(blk > lo, blk, jnp.inf), jnp.minimum))

    acc = lax.fori_loop(0, n_chunks, body, jnp.full((SUBLANES, tq), jnp.inf, F32))
    return jnp.min(acc, axis=0, keepdims=True)


def _bracket_ends(st_ref, lo, hi, n_chunks):
    tq = st_ref.shape[1]

    def body(c, carry):
        bot, top = carry
        off = pl.multiple_of(c * KEY_CHUNK, KEY_CHUNK)
        blk = st_ref[pl.ds(off, KEY_CHUNK), :]
        bot = jnp.minimum(bot, _fold_rows(jnp.where(blk > lo, blk, jnp.inf), jnp.minimum))
        top = jnp.maximum(top, _fold_rows(jnp.where(blk <= hi, blk, NEG_INF), jnp.maximum))
        return bot, top

    bot, top = lax.fori_loop(0, n_chunks, body, (jnp.full((SUBLANES, tq), jnp.inf, F32),
                                                jnp.full((SUBLANES, tq), NEG_INF, F32)))
    return jnp.min(bot, axis=0, keepdims=True), jnp.max(top, axis=0, keepdims=True)


def _dsa_kernel(q_ref, qs_ref, wt_ref, kk_ref, k_ref, vt_ref, o_ref,
                st_ref, bias_ref, qsr_ref, qaug_ref, acc_ref, lg_ref, *, seq_len, top_k):
    tq = q_ref.shape[0]
    j = pl.program_id(1)
    q0 = j * tq
    n_chunks = (q0 + tq + KEY_CHUNK - 1) // KEY_CHUNK
    rep = N_HEADS // N_KV_HEADS
    k_sel = float(top_k)
    q_pos = q0 + lax.broadcasted_iota(jnp.int32, (1, tq), 1)

    for hd in range(N_IDX_HEADS):
        qsr_ref[hd * tq:(hd + 1) * tq, :] = qs_ref[:, hd * IDX_W:(hd + 1) * IDX_W]
    eye = (lax.broadcasted_iota(jnp.int32, (tq, tq), 0)
           == lax.broadcasted_iota(jnp.int32, (tq, tq), 1)).astype(BF16)
    for g in range(N_KV_HEADS):
        for r in range(rep):
            hd = g * rep + r
            qaug_ref[g, r * tq:(r + 1) * tq, 0:HEAD_DIM] = q_ref[:, hd * HEAD_DIM:(hd + 1) * HEAD_DIM]
            qaug_ref[g, r * tq:(r + 1) * tq, HEAD_DIM:] = eye

    w_t = wt_ref[...]

    def score_body(i, carry):
        mn, mx, s1, s2 = carry
        for u in range(2):
            c = jnp.minimum(2 * i + u, n_chunks - 1)
            off = pl.multiple_of(c * KEY_CHUNK, KEY_CHUNK)
            dots = lax.dot_general(kk_ref[pl.ds(off, KEY_CHUNK), :], qsr_ref[...], NT_DIMS,
                                   preferred_element_type=F32)
            sc = w_t[0:1, :] * jnp.maximum(dots[:, 0:tq], 0.0)
            for hd in range(1, N_IDX_HEADS):
                sc = sc + w_t[hd:hd + 1, :] * jnp.maximum(dots[:, hd * tq:(hd + 1) * tq], 0.0)
            st_ref[pl.ds(off, KEY_CHUNK), :] = sc
            mn = jnp.minimum(mn, _fold_rows(sc, jnp.minimum))
            mx = jnp.maximum(mx, _fold_rows(sc, jnp.maximum))
            fresh = jnp.where(2 * i + u < n_chunks, 1.0, 0.0)
            for r in range(0, KEY_CHUNK, SAMPLE_STRIDE * SUBLANES):
                smp = sc[r:r + SUBLANES, :]
                s1 = s1 + fresh * smp
                s2 = s2 + fresh * (smp * smp)
        return mn, mx, s1, s2

    zeros8 = jnp.zeros((SUBLANES, tq), F32)
    mn, mx, s1, s2 = lax.fori_loop(
        0, (n_chunks + 1) // 2, score_body,
        (jnp.full((SUBLANES, tq), jnp.inf, F32), jnp.full((SUBLANES, tq), NEG_INF, F32), zeros8, zeros8))
    row_min = jnp.min(mn, axis=0, keepdims=True)
    row_max = jnp.max(mx, axis=0, keepdims=True)
    n_smp = (n_chunks * (KEY_CHUNK // SAMPLE_STRIDE)).astype(F32)
    smp_mean = jnp.sum(s1, axis=0, keepdims=True) / n_smp
    smp_var = jnp.maximum(jnp.sum(s2, axis=0, keepdims=True) / n_smp - smp_mean * smp_mean, 0.0)

    meta_pos = lax.broadcasted_iota(jnp.int32, (N_META, tq), 0)
    st_ref[0:N_META, :] = jnp.where(meta_pos <= q_pos, jnp.inf, NEG_INF)

    def causal_body(blk, carry):
        off = pl.multiple_of(blk * tq, tq)
        key_pos = off + lax.broadcasted_iota(jnp.int32, (tq, tq), 0)
        keep = (key_pos <= q_pos) & (key_pos < seq_len)
        st_ref[pl.ds(off, tq), :] = jnp.where(keep, st_ref[pl.ds(off, tq), :], NEG_INF)
        return carry

    lax.fori_loop(j, n_chunks * (KEY_CHUNK // tq), causal_body, 0)

    select_all = ((q_pos + 1) <= top_k) | (q_pos >= seq_len)

    def search():
        lo0 = jnp.where(select_all, 0.0, jnp.minimum(row_min, 0.0) * 2.0 - 1.0)
        hi0 = jnp.where(select_all, 0.0, row_max)

        def bisect(_, carry):
            lo, hi = carry
            mid = 0.5 * lo + 0.5 * hi
            ge = _count_above(st_ref, mid, n_chunks) >= k_sel
            return jnp.where(ge, mid, lo), jnp.where(ge, hi, mid)

        def snap(lo, hi):
            cand = _min_above(st_ref, lo, n_chunks)
            done = select_all | (_count_above(st_ref, cand, n_chunks) < k_sel)
            return jnp.where(done, lo, cand), hi, cand, jnp.sum(jnp.where(done, 0.0, 1.0))

        def more(state):
            lo, hi = lax.fori_loop(0, 4, bisect, state[:2])
            return snap(lo, hi)

        n_vis = jnp.minimum(q_pos + 1, seq_len).astype(F32)
        n_meta = jnp.minimum(q_pos + 1, N_META).astype(F32)
        tail = jnp.clip((k_sel - n_meta - 0.5) / jnp.maximum(n_vis - n_meta, 1.0), 1e-6, 1.0 - 1e-6)
        tt = jnp.sqrt(-2.0 * jnp.log(jnp.minimum(tail, 1.0 - tail)))
        z = tt - ((0.010328 * tt + 0.802853) * tt + 2.515517) / (
            ((0.001308 * tt + 0.189269) * tt + 1.432788) * tt + 1.0)
        guess = smp_mean + jnp.where(tail < 0.5, z, -z) * jnp.sqrt(smp_var)
        target = k_sel - 0.5

        def wide(c_lo, c_hi):
            return jnp.sum(jnp.where(jnp.logical_not(select_all) & (c_lo - c_hi > 2.0), 1.0, 0.0))

        def falsi(state):
            lo, hi, c_lo, c_hi, f_lo, f_hi, side, it = state
            t = jnp.where(it == 0, guess, lo + (hi - lo) * (f_lo / (f_lo - f_hi)))
            t = jnp.where((t > lo) & (t < hi), t, 0.5 * lo + 0.5 * hi)
            c = _count_above(st_ref, t, n_chunks)
            ge = c >= k_sel
            f = c - target
            f_lo = jnp.where(ge, f, jnp.where(side < 0.0, 0.5 * f_lo, f_lo))
            f_hi = jnp.where(ge, jnp.where(side > 0.0, 0.5 * f_hi, f_hi), f)
            lo, c_lo = jnp.where(ge, t, lo), jnp.where(ge, c, c_lo)
            hi, c_hi = jnp.where(ge, hi, t), jnp.where(ge, c_hi, c)
            return lo, hi, c_lo, c_hi, f_lo, f_hi, jnp.where(ge, 1.0, -1.0), it + 1

        def falsi_run(steps, state):
            state = lax.fori_loop(0, steps, lambda _, s: falsi(s), state)
            return state, wide(state[2], state[3])

        c_lo0, c_hi0 = n_vis, n_meta
        state = (lo0, hi0, c_lo0, c_hi0, c_lo0 - target, c_hi0 - target, jnp.zeros((1, tq), F32),
                 jnp.int32(0))
        state, n_wide = lax.while_loop(
            lambda s: (s[0][7] < MAX_FALSI_STEPS) & (s[1] > 0.0),
            lambda s: falsi_run(FALSI_RUN, s[0]), falsi_run(FALSI_FIRST_RUN, state))
        lo, hi, c_lo, c_hi = state[:4]
        bot, top = _bracket_ends(st_ref, lo, hi, n_chunks)
        is_top = (k_sel - c_hi) <= 1.0
        single = bot == top
        kth2 = jnp.where(is_top, top, bot)
        n_gt2 = jnp.where(single | is_top, c_hi, c_hi + 1.0)
        n_ge2 = jnp.where(single | jnp.logical_not(is_top), c_lo, c_hi + 1.0)

        def robust():
            narrow = select_all | (c_lo - c_hi <= 1.0)
            done = narrow | (_count_above(st_ref, bot, n_chunks) < k_sel)
            first = (jnp.where(done, lo, bot), hi, bot, jnp.sum(jnp.where(done, 0.0, 1.0)))
            kth = lax.while_loop(lambda s: s[3] > 0.0, more, first)[2]
            return kth, _count_above(st_ref, kth, n_chunks), _count_at_least(st_ref, kth, n_chunks)

        kth, n_gt, n_ge = lax.cond(n_wide > 0.0, robust, lambda: (kth2, n_gt2, n_ge2))

        thr = jnp.where(select_all, F32_LOWEST, kth)
        eq_take = jnp.where(select_all | (n_ge <= k_sel), jnp.inf, k_sel - n_gt)
        return thr, eq_take

    def keep_all():
        return jnp.full((1, tq), F32_LOWEST, F32), jnp.full((1, tq), jnp.inf, F32)

    thr, eq_take = lax.cond(q0 + tq > top_k, search, keep_all)
    any_tie = jnp.max(jnp.where(eq_take < jnp.inf, 1.0, 0.0)) > 0.0

    @pl.when(jnp.logical_not(any_tie))
    def _():
        def bias_body(c, carry):
            off = pl.multiple_of(c * KEY_CHUNK, KEY_CHUNK)
            sel = st_ref[pl.ds(off, KEY_CHUNK), :] >= thr
            bias_ref[pl.ds(off, KEY_CHUNK), :] = jnp.where(sel, 0.0, MASK_BIAS).astype(BF16)
            return carry

        lax.fori_loop(0, n_chunks, bias_body, 0)

    @pl.when(any_tie)
    def _():
        tri = (lax.broadcasted_iota(jnp.int32, (KEY_CHUNK, KEY_CHUNK), 1)
               <= lax.broadcasted_iota(jnp.int32, (KEY_CHUNK, KEY_CHUNK), 0)).astype(BF16)

        def tie_body(c, seen):
            off = pl.multiple_of(c * KEY_CHUNK, KEY_CHUNK)
            sc = st_ref[pl.ds(off, KEY_CHUNK), :]
            eq = jnp.where(sc == thr, 1.0, 0.0)
            rank = seen + jnp.dot(tri, eq.astype(BF16), preferred_element_type=F32)
            sel = (sc > thr) | ((sc == thr) & (rank <= eq_take))
            bias_ref[pl.ds(off, KEY_CHUNK), :] = jnp.where(sel, 0.0, MASK_BIAS).astype(BF16)
            return seen + jnp.sum(eq, axis=0, keepdims=True)

        lax.fori_loop(0, n_chunks, tie_body, jnp.zeros((1, tq), F32))

    acc_ref[...] = jnp.zeros(acc_ref.shape, F32)

    def qk(c, slot):
        off = pl.multiple_of(c * KEY_CHUNK, KEY_CHUNK)
        bias = bias_ref[pl.ds(off, KEY_CHUNK), :]
        for g in range(N_KV_HEADS):
            k_g = k_ref[pl.ds(off, KEY_CHUNK), g * HEAD_DIM:(g + 1) * HEAD_DIM]
            lg_ref[slot, g] = lax.dot_general(jnp.concatenate([k_g, bias], axis=1), qaug_ref[g], NT_DIMS,
                                              preferred_element_type=F32)

    def softmax_pv(c, slot, stats):
        off = pl.multiple_of(c * KEY_CHUNK, KEY_CHUNK)
        out = []
        for g in range(N_KV_HEADS):
            m_old, l_old = stats[2 * g], stats[2 * g + 1]
            logits = lg_ref[slot, g]
            m_new = jnp.maximum(m_old, jnp.max(logits, axis=0, keepdims=True))
            p = jnp.exp2(logits - m_new)
            alpha = jnp.exp2(m_old - m_new)
            l_new = alpha * l_old + jnp.sum(p, axis=0, keepdims=True)
            pv = jnp.dot(vt_ref[g, :, pl.ds(off, KEY_CHUNK)], p.astype(BF16), preferred_element_type=F32)
            acc_ref[g] = alpha * acc_ref[g] + pv
            out += [m_new, l_new]
        return tuple(out)

    last = n_chunks - 1
    qk(0, 0)

    def attn_body(i, stats):
        c = 2 * i
        qk(jnp.minimum(c + 1, last), 1)
        stats = softmax_pv(c, 0, stats)

        def second_half(stats):
            qk(jnp.minimum(c + 2, last), 0)
            return softmax_pv(c + 1, 1, stats)

        return lax.cond(c + 1 < n_chunks, second_half, lambda s: s, stats)

    init = (jnp.full((1, rep * tq), NEG_INF, F32), jnp.zeros((1, rep * tq), F32)) * N_KV_HEADS
    stats = lax.fori_loop(0, (n_chunks + 1) // 2, attn_body, init)

    for g in range(N_KV_HEADS):
        o_t = acc_ref[g] / stats[2 * g + 1]
        for r in range(rep):
            hd = g * rep + r
            o_ref[:, hd * HEAD_DIM:(hd + 1) * HEAD_DIM] = o_t[:, r * tq:(r + 1) * tq].T.astype(o_ref.dtype)


def _dsa(q, qs, w_t, kk, k, v_t, seq_len, top_k):
    bsz, lp, dq = q.shape
    lk = kk.shape[1]
    rep = N_HEADS // N_KV_HEADS
    qrow = lambda w: pl.BlockSpec((None, Q_TILE, w), lambda b, j: (b, j, 0))
    keys = lambda w: pl.BlockSpec((None, lk, w), lambda b, j: (b, 0, 0))
    kern = functools.partial(_dsa_kernel, seq_len=seq_len, top_k=top_k)
    return pl.pallas_call(
        kern,
        grid=(bsz, lp // Q_TILE),
        in_specs=[qrow(dq), qrow(qs.shape[-1]),
                  pl.BlockSpec((None, N_IDX_HEADS, Q_TILE), lambda b, j: (b, 0, j)),
                  keys(IDX_W), keys(k.shape[-1]),
                  pl.BlockSpec((None, N_KV_HEADS, HEAD_DIM, lk), lambda b, j: (b, 0, 0, 0))],
        out_specs=qrow(dq),
        out_shape=jax.ShapeDtypeStruct((bsz, lp, dq), BF16),
        scratch_shapes=[pltpu.VMEM((lk, Q_TILE), F32), pltpu.VMEM((lk, Q_TILE), BF16),
                        pltpu.VMEM((N_IDX_HEADS * Q_TILE, IDX_W), BF16),
                        pltpu.VMEM((N_KV_HEADS, rep * Q_TILE, 2 * HEAD_DIM), BF16),
                        pltpu.VMEM((N_KV_HEADS, HEAD_DIM, rep * Q_TILE), F32),
                        pltpu.VMEM((2, N_KV_HEADS, KEY_CHUNK, rep * Q_TILE), F32)],
        compiler_params=pltpu.CompilerParams(
            dimension_semantics=("parallel", "arbitrary"), vmem_limit_bytes=VMEM_LIMIT),
        name="dsa_attention",
    )(q, qs, w_t, kk, k, v_t)


def _merge_ffn_kernel(s_ref, ya_ref, ao_ref, gate_ref, wo_ref, wm_ref, fng_ref, up_ref, dww_ref,
                      dwb_ref, down_ref, fin_ref, out_ref, carry_ref, ubuf_ref, hb_ref, acc_ref,
                      obuf_ref, out_sem, *, seq_len, n_b, n_t):
    tm, d = s_ref.shape
    d_ff = down_ref.shape[0]
    b, i = pl.program_id(0), pl.program_id(1)

    @pl.when(i == 0)
    def _():
        carry_ref[...] = jnp.zeros(carry_ref.shape, F32)

    y_b = jnp.dot(ao_ref[...], wo_ref[...], preferred_element_type=F32)
    mix = gate_ref[:, :d].astype(F32) * ya_ref[...].astype(F32) + gate_ref[:, d:].astype(F32) * y_b
    stream = s_ref[...] + jnp.dot(mix.astype(BF16), wm_ref[...], preferred_element_type=F32)
    h = stream * lax.rsqrt(jnp.mean(stream * stream, axis=-1, keepdims=True) + EPS) * fng_ref[...]
    hb_ref[...] = h.astype(BF16)
    acc_ref[...] = stream

    def conv3(u, col, slot):
        ubuf_ref[slot, 0:SUBLANES, :] = carry_ref[:, pl.ds(col, FFN_CHUNK)]
        ubuf_ref[slot, SUBLANES:, :] = u
        carry_ref[:, pl.ds(col, FFN_CHUNK)] = u[tm - SUBLANES:, :]
        w = dww_ref[:, pl.ds(col, FFN_CHUNK)]
        out = dwb_ref[:, pl.ds(col, FFN_CHUNK)] + w[2:3, :] * u
        out = out + w[1:2, :] * ubuf_ref[slot, SUBLANES - 1:SUBLANES - 1 + tm, :]
        return out + w[0:1, :] * ubuf_ref[slot, SUBLANES - 2:SUBLANES - 2 + tm, :]

    def chunk(c, carry):
        col_a = pl.multiple_of(c * FFN_CHUNK, FFN_CHUNK)
        col_b = pl.multiple_of(d_ff + c * FFN_CHUNK, FFN_CHUNK)
        hb = hb_ref[...]
        a = conv3(jnp.dot(hb, up_ref[:, pl.ds(col_a, FFN_CHUNK)], preferred_element_type=F32), col_a, 0)
        b = conv3(jnp.dot(hb, up_ref[:, pl.ds(col_b, FFN_CHUNK)], preferred_element_type=F32), col_b, 1)
        act = (a * jax.nn.sigmoid(a) * b).astype(BF16)
        acc_ref[...] += jnp.dot(act, down_ref[pl.ds(col_a, FFN_CHUNK), :], preferred_element_type=F32)
        return carry

    lax.fori_loop(0, d_ff // FFN_CHUNK, chunk, 0, unroll=True)

    step = b * n_t + i
    slot = step % 2
    y = acc_ref[...]
    obuf_ref[slot] = y * lax.rsqrt(jnp.mean(y * y, axis=-1, keepdims=True) + EPS) * fin_ref[...]

    def tile_copy(kind, bb, tile, buf):
        if kind == "first":
            src0, rows, dst0 = N_META, min(tm, seq_len) - N_META, 0
        elif kind == "last":
            src0, rows, dst0 = 0, seq_len - (n_t - 1) * tm, (n_t - 1) * tm - N_META
        else:
            src0, rows, dst0 = 0, tm, pl.multiple_of(tile * tm - N_META, SUBLANES)
        return pltpu.make_async_copy(obuf_ref.at[buf, pl.ds(src0, rows)],
                                     out_ref.at[bb, pl.ds(dst0, rows)], out_sem.at[buf])

    def per_kind(tile, fn):
        pl.when(tile == 0)(lambda: fn("first"))
        if n_t > 2:
            pl.when((tile > 0) & (tile < n_t - 1))(lambda: fn("middle"))
        if n_t > 1:
            pl.when(tile == n_t - 1)(lambda: fn("last"))

    per_kind(i, lambda kind: tile_copy(kind, b, i, slot).start())

    @pl.when(step > 0)
    def _():
        prev_tile = jnp.where(i == 0, n_t - 1, i - 1)
        prev_b = jnp.where(i == 0, b - 1, b)
        per_kind(prev_tile, lambda kind: tile_copy(kind, prev_b, prev_tile, 1 - slot).wait())

    @pl.when(step == n_b * n_t - 1)
    def _():
        per_kind(i, lambda kind: tile_copy(kind, b, i, slot).wait())


def _merge_ffn(stream, ya, ao, gates, wo, wm, fng, up, dww, dwb, down, fin, tm, seq_len):
    bsz, lp, d = stream.shape
    row = lambda w: pl.BlockSpec((None, tm, w), lambda b, i: (b, i, 0))
    full = lambda a: pl.BlockSpec(a.shape, lambda b, i: (0,) * a.ndim, pipeline_mode=pl.Buffered(1))
    return pl.pallas_call(
        functools.partial(_merge_ffn_kernel, seq_len=seq_len, n_b=bsz, n_t=lp // tm),
        grid=(bsz, lp // tm),
        in_specs=[row(d), row(d), row(d), row(2 * d), full(wo), full(wm), full(fng), full(up),
                  full(dww), full(dwb), full(down), full(fin)],
        out_specs=pl.BlockSpec(memory_space=pl.ANY),
        out_shape=jax.ShapeDtypeStruct((bsz, seq_len - N_META, d), F32),
        scratch_shapes=[pltpu.VMEM((SUBLANES, up.shape[1]), F32),
                        pltpu.VMEM((2, SUBLANES + tm, FFN_CHUNK), F32),
                        pltpu.VMEM((tm, d), BF16), pltpu.VMEM((tm, d), F32),
                        pltpu.VMEM((2, tm, d), F32), pltpu.SemaphoreType.DMA((2,))],
        compiler_params=pltpu.CompilerParams(
            dimension_semantics=("arbitrary", "arbitrary"), vmem_limit_bytes=VMEM_LIMIT),
        name="merge_ffn",
    )(stream, ya, ao, gates, wo, wm, fng, up, dww, dwb, down, fin)


def _row_tile(lp, prefer=ROW_TILE):
    for t in (prefer, ROW_TILE, 256, LANES):
        if lp % t == 0:
            return t
    raise ValueError(f"padded length {lp} is not a multiple of {LANES}")


def kernel(x, meta_tokens, mix_norm_g, w_in, conv_ln_g, conv_ln_b, conv_dw_w, conv_dw_b, conv_pw_out,
           attn_w_o, w_merge_out, ffn_norm_g, ffn_up, ffn_dw_w, ffn_dw_b, ffn_down, final_norm_g):
    bsz, seq, d = x.shape
    depth = w_in.shape[0]
    seq_len = seq + N_META
    lp = -(-seq_len // LANES) * LANES
    lk = -(-lp // KEY_CHUNK) * KEY_CHUNK
    tm = _row_tile(lp)
    top_k = min(TOPK_MAX, seq_len // 4)
    assert top_k > N_META and conv_dw_w.shape[1] == CONV_WIDTH and ffn_dw_w.shape[1] == FFN_CONV_WIDTH
    d_conv = conv_pw_out.shape[1]
    d_ff = ffn_down.shape[1]
    assert d_ff % FFN_CHUNK == 0 and d % LANES == 0

    meta = jnp.broadcast_to(meta_tokens.astype(x.dtype)[None], (bsz, N_META, d))
    stream = jnp.concatenate([meta, x, jnp.zeros((bsz, lp - seq_len, d), x.dtype)], axis=1)
    tabs = _rope_tables(lp, HEAD_DIM, 1) + _rope_tables(lp, IDX_DIM, LANES // IDX_DIM)

    n_q, n_kv = N_HEADS * HEAD_DIM, N_KV_HEADS * HEAD_DIM
    n_qi = N_IDX_HEADS * IDX_DIM
    edges = [0, 2 * d_conv]
    for wdt in (n_q, n_kv, n_kv, n_qi, IDX_DIM, N_IDX_HEADS, 2 * d):
        edges.append(edges[-1] + wdt)
    row2 = lambda a: a.reshape(1, -1).astype(F32)
    pad_keys = lambda a: jnp.pad(a, ((0, 0), (0, lk - lp), (0, 0)))

    for l in range(depth):
        w = w_in[l]
        wu = w[:, edges[0]:edges[1]].astype(BF16)
        wq = (w[:, edges[1]:edges[2]] * (HEAD_DIM ** -0.5 * math.log2(math.e))).astype(BF16)
        wkv = w[:, edges[2]:edges[4]].astype(BF16)
        wqi = w[:, edges[4]:edges[5]].astype(BF16)
        wkw = jnp.pad(w[:, edges[5]:edges[7]], ((0, 0), (0, LANES - IDX_DIM - N_IDX_HEADS))).astype(BF16)
        wg = w[:, edges[7]:edges[8]].astype(BF16)

        ya, q, k, v, qs, kk, w_idx, gates = _in_proj(
            stream, row2(mix_norm_g[l]), wu, wq, wkv, wqi, wkw, wg, conv_dw_w[l],
            row2(conv_dw_b[l]), row2(conv_ln_g[l]), row2(conv_ln_b[l]), conv_pw_out[l].astype(BF16), tabs, tm)
        v_t = jnp.swapaxes(pad_keys(v).reshape(bsz, lk, N_KV_HEADS, HEAD_DIM), 1, 3).swapaxes(1, 2)
        ao = _dsa(q, qs, jnp.swapaxes(w_idx, 1, 2), pad_keys(kk), pad_keys(k), v_t, seq_len, top_k)
        assert l == depth - 1, "merge_ffn applies the final norm and drops the meta rows: single layer only"
        out = _merge_ffn(stream, ya, ao, gates, attn_w_o[l].astype(BF16), w_merge_out[l].astype(BF16),
                         row2(ffn_norm_g[l]), ffn_up[l].astype(BF16), ffn_dw_w[l], row2(ffn_dw_b[l]),
                         ffn_down[l].astype(BF16), row2(final_norm_g), _row_tile(lp, FFN_ROW_TILE), seq_len)
    return out
```

```python
import functools
import math

import jax
import jax.numpy as jnp
from jax import lax
from jax.experimental import pallas as pl
from jax.experimental.pallas import tpu as pltpu

N_META = 16
CONV_WIDTH = 31
N_HEADS = 8
HEAD_DIM = 128
N_KV_HEADS = 2
N_IDX_HEADS = 8
IDX_DIM = 64
TOPK_MAX = 256
FFN_CONV_WIDTH = 3
ROPE_THETA = 10000.0
EPS = 1e-6

LANES = 128
SUBLANES = 8
Q_TILE = 128
KEY_CHUNK = 512
ROW_TILE = 384
FFN_ROW_TILE = 704
CONV_HALO = 32
CONV_ROWS = 64
FFN_CHUNK = 256
IDX_W = LANES
SAMPLE_STRIDE = 8
FALSI_FIRST_RUN = 12
FALSI_RUN = 2
MAX_FALSI_STEPS = 18
VMEM_LIMIT = 60 * 1024 * 1024

F32 = jnp.float32
BF16 = jnp.bfloat16
NEG_INF = float("-inf")
MASK_BIAS = -1e30
F32_LOWEST = float(jnp.finfo(jnp.float32).min)
NT_DIMS = (((1,), (1,)), ((), ()))


def _rope_tables(n_pos, dim, reps):
    half = dim // 2
    inv = ROPE_THETA ** (-jnp.arange(half, dtype=F32) / half)
    ang = jnp.arange(n_pos, dtype=jnp.int32).astype(F32)[:, None] * inv[None, :]
    cos, sin = jnp.cos(ang), jnp.sin(ang)
    cos_f = jnp.concatenate([cos, cos], axis=-1)
    sin_f = jnp.concatenate([-sin, sin], axis=-1)
    return jnp.tile(cos_f, (1, reps)), jnp.tile(sin_f, (1, reps))


def _rope128(x, cos, sin):
    return x * cos + pltpu.roll(x, HEAD_DIM // 2, axis=1) * sin


def _rope64(x, cos, sin, first_half):
    partner = jnp.where(first_half, pltpu.roll(x, LANES - IDX_DIM // 2, axis=1),
                        pltpu.roll(x, IDX_DIM // 2, axis=1))
    return x * cos + partner * sin


def _conv_tile(glu, dww_ref, dwb_ref, buf_ref, conv_ref, interleave):
    tt, c = glu.shape

    @pl.when(pl.program_id(1) == 0)
    def _():
        buf_ref[0:CONV_HALO, :] = jnp.zeros((CONV_HALO, c), F32)

    @pl.when(pl.program_id(1) > 0)
    def _():
        buf_ref[0:CONV_HALO, :] = buf_ref[tt:tt + CONV_HALO, :]

    buf_ref[CONV_HALO:, :] = glu

    base = CONV_HALO - (CONV_WIDTH - 1)
    n_cb = c // LANES
    for cb in range(n_cb):
        cs = slice(cb * LANES, (cb + 1) * LANES)
        w_c = dww_ref[:, cs]
        b_c = dwb_ref[:, cs]
        for rb in range(tt // CONV_ROWS):
            r0 = rb * CONV_ROWS
            acc = jnp.broadcast_to(b_c, (CONV_ROWS, LANES))
            for ph in range(SUBLANES):
                rows = CONV_ROWS + (SUBLANES if ph else 0)
                part = None
                for a in range((base + CONV_WIDTH - 1) // SUBLANES + 1):
                    j = SUBLANES * a + ph - base
                    if 0 <= j < CONV_WIDTH:
                        term = buf_ref[r0 + SUBLANES * a:r0 + SUBLANES * a + rows, cs] * w_c[j:j + 1, :]
                        part = term if part is None else part + term
                acc = acc + part[ph:ph + CONV_ROWS, :]
            conv_ref[r0:r0 + CONV_ROWS, cs] = acc
        for piece in interleave[cb::n_cb]:
            piece()


def _in_proj_kernel(x_ref, g_ref, wu_ref, wq_ref, wkv_ref, wqi_ref, wkw_ref, wg_ref,
                    dww_ref, dwb_ref, lng_ref, lnb_ref, pw_ref, c128_ref, s128_ref, c64_ref, s64_ref,
                    ya_ref, q_ref, k_ref, v_ref, qs_ref, kk_ref, w_ref, gate_ref, buf_ref, conv_ref):
    d_conv = buf_ref.shape[-1]
    kv_w = k_ref.shape[-1]
    x = x_ref[...]
    h = x * lax.rsqrt(jnp.mean(x * x, axis=-1, keepdims=True) + EPS) * g_ref[...]
    hb = h.astype(BF16)
    dot = functools.partial(jnp.dot, preferred_element_type=F32)

    c128, s128 = c128_ref[...], s128_ref[...]
    c64, s64 = c64_ref[...], s64_ref[...]
    lane = lax.broadcasted_iota(jnp.int32, (x.shape[0], LANES), 1)
    first_half = (lane & (IDX_DIM - 1)) < (IDX_DIM // 2)
    low_head = lane < IDX_DIM

    def gate_piece(c0, c1):
        def piece():
            gate_ref[:, c0:c1] = jax.nn.sigmoid(dot(hb, wg_ref[:, c0:c1])).astype(gate_ref.dtype)
        return piece

    def q_piece(h0, h1):
        def piece():
            q = dot(hb, wq_ref[:, h0 * HEAD_DIM:h1 * HEAD_DIM])
            for hd in range(h1 - h0):
                q_ref[:, (h0 + hd) * HEAD_DIM:(h0 + hd + 1) * HEAD_DIM] = _rope128(
                    q[:, hd * HEAD_DIM:(hd + 1) * HEAD_DIM], c128, s128).astype(q_ref.dtype)
        return piece

    def kv_piece():
        kv = dot(hb, wkv_ref[...])
        for hd in range(kv_w // HEAD_DIM):
            sl = slice(hd * HEAD_DIM, (hd + 1) * HEAD_DIM)
            k_ref[:, sl] = _rope128(kv[:, sl], c128, s128).astype(k_ref.dtype)
        v_ref[...] = kv[:, kv_w:].astype(v_ref.dtype)

    def qi_piece(b0, b1):
        def piece():
            qi = dot(hb, wqi_ref[:, b0 * LANES:b1 * LANES])
            for blk in range(b1 - b0):
                xr = _rope64(qi[:, blk * LANES:(blk + 1) * LANES], c64, s64, first_half)
                xs = pltpu.roll(xr, IDX_DIM, axis=1)
                for sub, head in enumerate((xr, xs)):
                    c0 = (2 * (b0 + blk) + sub) * IDX_W
                    qs_ref[:, c0:c0 + IDX_W] = jnp.where(low_head, head, 0.0).astype(BF16)
        return piece

    def kw_piece():
        kw = dot(hb, wkw_ref[...])
        kk_ref[...] = jnp.where(low_head, _rope64(kw, c64, s64, first_half), 0.0).astype(BF16)
        w_ref[...] = kw[:, IDX_DIM:IDX_DIM + N_IDX_HEADS] * (N_IDX_HEADS ** -0.5 * IDX_DIM ** -0.5)

    n_gate, n_qh, n_qi = gate_ref.shape[-1], q_ref.shape[-1] // HEAD_DIM, wqi_ref.shape[-1] // LANES
    pieces = [gate_piece(0, n_gate // 4), gate_piece(n_gate // 4, n_gate // 2),
              gate_piece(n_gate // 2, 3 * n_gate // 4), gate_piece(3 * n_gate // 4, n_gate),
              q_piece(0, n_qh // 2), q_piece(n_qh // 2, n_qh), kv_piece,
              qi_piece(0, n_qi // 2), qi_piece(n_qi // 2, n_qi), kw_piece]

    u = dot(hb, wu_ref[...])
    _conv_tile(u[:, :d_conv] * jax.nn.sigmoid(u[:, d_conv:]), dww_ref, dwb_ref, buf_ref, conv_ref, pieces)
    hc = conv_ref[...]
    mu = jnp.mean(hc, axis=-1, keepdims=True)
    dlt = hc - mu
    var = jnp.mean(dlt * dlt, axis=-1, keepdims=True)
    hn = dlt * lax.rsqrt(var + EPS) * lng_ref[...] + lnb_ref[...]
    hs = hn * jax.nn.sigmoid(hn)
    ya_ref[...] = dot(hs.astype(BF16), pw_ref[...]).astype(ya_ref.dtype)


def _in_proj(stream, g, wu, wq, wkv, wqi, wkw, wg, dww, dwb, lng, lnb, pw, tabs, tm):
    bsz, lp, d = stream.shape
    nt = lp // tm
    d_conv = wu.shape[1] // 2
    row = lambda w: pl.BlockSpec((None, tm, w), lambda b, i: (b, i, 0))
    full = lambda a: pl.BlockSpec(a.shape, lambda b, i: (0,) * a.ndim, pipeline_mode=pl.Buffered(1))
    tab = pl.BlockSpec((tm, LANES), lambda b, i: (i, 0))
    outs = [(pw.shape[1], BF16), (wq.shape[1], BF16), (wkv.shape[1] // 2, BF16),
            (wkv.shape[1] // 2, BF16), (N_IDX_HEADS * IDX_W, BF16), (IDX_W, BF16),
            (N_IDX_HEADS, F32), (wg.shape[1], BF16)]
    weights = (g, wu, wq, wkv, wqi, wkw, wg, dww, dwb, lng, lnb, pw)
    return pl.pallas_call(
        _in_proj_kernel,
        grid=(bsz, nt),
        in_specs=[row(d)] + [full(a) for a in weights] + [tab, tab, tab, tab],
        out_specs=[row(w) for w, _ in outs],
        out_shape=[jax.ShapeDtypeStruct((bsz, lp, w), dt) for w, dt in outs],
        scratch_shapes=[pltpu.VMEM((CONV_HALO + tm, d_conv), F32), pltpu.VMEM((tm, d_conv), F32)],
        compiler_params=pltpu.CompilerParams(
            dimension_semantics=("parallel", "arbitrary"), vmem_limit_bytes=VMEM_LIMIT),
        name="in_proj_conv",
    )(stream, *weights, *tabs)


def _fold_rows(x, op):
    parts = [x[r * SUBLANES:(r + 1) * SUBLANES, :] for r in range(x.shape[0] // SUBLANES)]
    while len(parts) > 1:
        nxt = [op(parts[i], parts[i + 1]) for i in range(0, len(parts) - 1, 2)]
        parts = nxt + parts[len(parts) - len(parts) % 2:]
    return parts[0]


def _count(st_ref, thr, n_chunks, strict):
    tq = st_ref.shape[1]

    def body(c, acc):
        off = pl.multiple_of(c * KEY_CHUNK, KEY_CHUNK)
        blk = st_ref[pl.ds(off, KEY_CHUNK), :]
        hit = (blk > thr) if strict else (blk >= thr)
        return acc + _fold_rows(jnp.where(hit, 1.0, 0.0), jnp.add)

    acc = lax.fori_loop(0, n_chunks, body, jnp.zeros((SUBLANES, tq), F32))
    return jnp.sum(acc, axis=0, keepdims=True)


def _count_above(st_ref, thr, n_chunks):
    return _count(st_ref, thr, n_chunks, True)


def _count_at_least(st_ref, thr, n_chunks):
    return _count(st_ref, thr, n_chunks, False)


def _min_above(st_ref, lo, n_chunks):
    tq = st_ref.shape[1]

    def body(c, acc):
        off = pl.multiple_of(c * KEY_CHUNK, KEY_CHUNK)
        blk = st_ref[pl.ds(off, KEY_CHUNK), :]
        return jnp.minimum(acc, _fold_rows(jnp.where(blk > lo, blk, jnp.inf), jnp.minimum))

    acc = lax.fori_loop(0, n_chunks, body, jnp.full((SUBLANES, tq), jnp.inf, F32))
    return jnp.min(acc, axis=0, keepdims=True)


def _bracket_ends(st_ref, lo, hi, n_chunks):
    tq = st_ref.shape[1]

    def body(c, carry):
        bot, top = carry
        off = pl.multiple_of(c * KEY_CHUNK, KEY_CHUNK)
        blk = st_ref[pl.ds(off, KEY_CHUNK), :]
        bot = jnp.minimum(bot, _fold_rows(jnp.where(blk > lo, blk, jnp.inf), jnp.minimum))
        top = jnp.maximum(top, _fold_rows(jnp.where(blk <= hi, blk, NEG_INF), jnp.maximum))
        return bot, top

    bot, top = lax.fori_loop(0, n_chunks, body, (jnp.full((SUBLANES, tq), jnp.inf, F32),
                                                jnp.full((SUBLANES, tq), NEG_INF, F32)))
    return jnp.min(bot, axis=0, keepdims=True), jnp.max(top, axis=0, keepdims=True)


def _dsa_kernel(q_ref, qs_ref, wt_ref, kk_ref, k_ref, vt_ref, o_ref,
                st_ref, bias_ref, qsr_ref, qaug_ref, acc_ref, lg_ref, *, seq_len, top_k):
    tq = q_ref.shape[0]
    j = pl.program_id(1)
    q0 = j * tq
    n_chunks = (q0 + tq + KEY_CHUNK - 1) // KEY_CHUNK
    rep = N_HEADS // N_KV_HEADS
    k_sel = float(top_k)
    q_pos = q0 + lax.broadcasted_iota(jnp.int32, (1, tq), 1)

    for hd in range(N_IDX_HEADS):
        qsr_ref[hd * tq:(hd + 1) * tq, :] = qs_ref[:, hd * IDX_W:(hd + 1) * IDX_W]
    eye = (lax.broadcasted_iota(jnp.int32, (tq, tq), 0)
           == lax.broadcasted_iota(jnp.int32, (tq, tq), 1)).astype(BF16)
    for g in range(N_KV_HEADS):
        for r in range(rep):
            hd = g * rep + r
            qaug_ref[g, r * tq:(r + 1) * tq, 0:HEAD_DIM] = q_ref[:, hd * HEAD_DIM:(hd + 1) * HEAD_DIM]
            qaug_ref[g, r * tq:(r + 1) * tq, HEAD_DIM:] = eye

    w_t = wt_ref[...]

    def score_body(i, carry):
        mn, mx, s1, s2 = carry
        for u in range(2):
            c = jnp.minimum(2 * i + u, n_chunks - 1)
            off = pl.multiple_of(c * KEY_CHUNK, KEY_CHUNK)
            dots = lax.dot_general(kk_ref[pl.ds(off, KEY_CHUNK), :], qsr_ref[...], NT_DIMS,
                                   preferred_element_type=F32)
            sc = w_t[0:1, :] * jnp.maximum(dots[:, 0:tq], 0.0)
            for hd in range(1, N_IDX_HEADS):
                sc = sc + w_t[hd:hd + 1, :] * jnp.maximum(dots[:, hd * tq:(hd + 1) * tq], 0.0)
            st_ref[pl.ds(off, KEY_CHUNK), :] = sc
            mn = jnp.minimum(mn, _fold_rows(sc, jnp.minimum))
            mx = jnp.maximum(mx, _fold_rows(sc, jnp.maximum))
            fresh = jnp.where(2 * i + u < n_chunks, 1.0, 0.0)
            for r in range(0, KEY_CHUNK, SAMPLE_STRIDE * SUBLANES):
                smp = sc[r:r + SUBLANES, :]
                s1 = s1 + fresh * smp
                s2 = s2 + fresh * (smp * smp)
        return mn, mx, s1, s2

    zeros8 = jnp.zeros((SUBLANES, tq), F32)
    mn, mx, s1, s2 = lax.fori_loop(
        0, (n_chunks + 1) // 2, score_body,
        (jnp.full((SUBLANES, tq), jnp.inf, F32), jnp.full((SUBLANES, tq), NEG_INF, F32), zeros8, zeros8))
    row_min = jnp.min(mn, axis=0, keepdims=True)
    row_max = jnp.max(mx, axis=0, keepdims=True)
    n_smp = (n_chunks * (KEY_CHUNK // SAMPLE_STRIDE)).astype(F32)
    smp_mean = jnp.sum(s1, axis=0, keepdims=True) / n_smp
    smp_var = jnp.maximum(jnp.sum(s2, axis=0, keepdims=True) / n_smp - smp_mean * smp_mean, 0.0)

    meta_pos = lax.broadcasted_iota(jnp.int32, (N_META, tq), 0)
    st_ref[0:N_META, :] = jnp.where(meta_pos <= q_pos, jnp.inf, NEG_INF)

    def causal_body(blk, carry):
        off = pl.multiple_of(blk * tq, tq)
        key_pos = off + lax.broadcasted_iota(jnp.int32, (tq, tq), 0)
        keep = (key_pos <= q_pos) & (key_pos < seq_len)
        st_ref[pl.ds(off, tq), :] = jnp.where(keep, st_ref[pl.ds(off, tq), :], NEG_INF)
        return carry

    lax.fori_loop(j, n_chunks * (KEY_CHUNK // tq), causal_body, 0)

    select_all = ((q_pos + 1) <= top_k) | (q_pos >= seq_len)

    def search():
        lo0 = jnp.where(select_all, 0.0, jnp.minimum(row_min, 0.0) * 2.0 - 1.0)
        hi0 = jnp.where(select_all, 0.0, row_max)

        def bisect(_, carry):
            lo, hi = carry
            mid = 0.5 * lo + 0.5 * hi
            ge = _count_above(st_ref, mid, n_chunks) >= k_sel
            return jnp.where(ge, mid, lo), jnp.where(ge, hi, mid)

        def snap(lo, hi):
            cand = _min_above(st_ref, lo, n_chunks)
            done = select_all | (_count_above(st_ref, cand, n_chunks) < k_sel)
            return jnp.where(done, lo, cand), hi, cand, jnp.sum(jnp.where(done, 0.0, 1.0))

        def more(state):
            lo, hi = lax.fori_loop(0, 4, bisect, state[:2])
            return snap(lo, hi)

        n_vis = jnp.minimum(q_pos + 1, seq_len).astype(F32)
        n_meta = jnp.minimum(q_pos + 1, N_META).astype(F32)
        tail = jnp.clip((k_sel - n_meta - 0.5) / jnp.maximum(n_vis - n_meta, 1.0), 1e-6, 1.0 - 1e-6)
        tt = jnp.sqrt(-2.0 * jnp.log(jnp.minimum(tail, 1.0 - tail)))
        z = tt - ((0.010328 * tt + 0.802853) * tt + 2.515517) / (
            ((0.001308 * tt + 0.189269) * tt + 1.432788) * tt + 1.0)
        guess = smp_mean + jnp.where(tail < 0.5, z, -z) * jnp.sqrt(smp_var)
        target = k_sel - 0.5

        def wide(c_lo, c_hi):
            return jnp.sum(jnp.where(jnp.logical_not(select_all) & (c_lo - c_hi > 2.0), 1.0, 0.0))

        def falsi(state):
            lo, hi, c_lo, c_hi, f_lo, f_hi, side, it = state
            t = jnp.where(it == 0, guess, lo + (hi - lo) * (f_lo / (f_lo - f_hi)))
            t = jnp.where((t > lo) & (t < hi), t, 0.5 * lo + 0.5 * hi)
            c = _count_above(st_ref, t, n_chunks)
            ge = c >= k_sel
            f = c - target
            f_lo = jnp.where(ge, f, jnp.where(side < 0.0, 0.5 * f_lo, f_lo))
            f_hi = jnp.where(ge, jnp.where(side > 0.0, 0.5 * f_hi, f_hi), f)
            lo, c_lo = jnp.where(ge, t, lo), jnp.where(ge, c, c_lo)
            hi, c_hi = jnp.where(ge, hi, t), jnp.where(ge, c_hi, c)
            return lo, hi, c_lo, c_hi, f_lo, f_hi, jnp.where(ge, 1.0, -1.0), it + 1

        def falsi_run(steps, state):
            state = lax.fori_loop(0, steps, lambda _, s: falsi(s), state)
            return state, wide(state[2], state[3])

        c_lo0, c_hi0 = n_vis, n_meta
        state = (lo0, hi0, c_lo0, c_hi0, c_lo0 - target, c_hi0 - target, jnp.zeros((1, tq), F32),
                 jnp.int32(0))
        state, n_wide = lax.while_loop(
            lambda s: (s[0][7] < MAX_FALSI_STEPS) & (s[1] > 0.0),
            lambda s: falsi_run(FALSI_RUN, s[0]), falsi_run(FALSI_FIRST_RUN, state))
        lo, hi, c_lo, c_hi = state[:4]
        bot, top = _bracket_ends(st_ref, lo, hi, n_chunks)
        is_top = (k_sel - c_hi) <= 1.0
        single = bot == top
        kth2 = jnp.where(is_top, top, bot)
        n_gt2 = jnp.where(single | is_top, c_hi, c_hi + 1.0)
        n_ge2 = jnp.where(single | jnp.logical_not(is_top), c_lo, c_hi + 1.0)

        def robust():
            narrow = select_all | (c_lo - c_hi <= 1.0)
            done = narrow | (_count_above(st_ref, bot, n_chunks) < k_sel)
            first = (jnp.where(done, lo, bot), hi, bot, jnp.sum(jnp.where(done, 0.0, 1.0)))
            kth = lax.while_loop(lambda s: s[3] > 0.0, more, first)[2]
            return kth, _count_above(st_ref, kth, n_chunks), _count_at_least(st_ref, kth, n_chunks)

        kth, n_gt, n_ge = lax.cond(n_wide > 0.0, robust, lambda: (kth2, n_gt2, n_ge2))

        thr = jnp.where(select_all, F32_LOWEST, kth)
        eq_take = jnp.where(select_all | (n_ge <= k_sel), jnp.inf, k_sel - n_gt)
        return thr, eq_take

    def keep_all():
        return jnp.full((1, tq), F32_LOWEST, F32), jnp.full((1, tq), jnp.inf, F32)

    thr, eq_take = lax.cond(q0 + tq > top_k, search, keep_all)
    any_tie = jnp.max(jnp.where(eq_take < jnp.inf, 1.0, 0.0)) > 0.0

    @pl.when(jnp.logical_not(any_tie))
    def _():
        def bias_body(c, carry):
            off = pl.multiple_of(c * KEY_CHUNK, KEY_CHUNK)
            sel = st_ref[pl.ds(off, KEY_CHUNK), :] >= thr
            bias_ref[pl.ds(off, KEY_CHUNK), :] = jnp.where(sel, 0.0, MASK_BIAS).astype(BF16)
            return carry

        lax.fori_loop(0, n_chunks, bias_body, 0)

    @pl.when(any_tie)
    def _():
        tri = (lax.broadcasted_iota(jnp.int32, (KEY_CHUNK, KEY_CHUNK), 1)
               <= lax.broadcasted_iota(jnp.int32, (KEY_CHUNK, KEY_CHUNK), 0)).astype(BF16)

        def tie_body(c, seen):
            off = pl.multiple_of(c * KEY_CHUNK, KEY_CHUNK)
            sc = st_ref[pl.ds(off, KEY_CHUNK), :]
            eq = jnp.where(sc == thr, 1.0, 0.0)
            rank = seen + jnp.dot(tri, eq.astype(BF16), preferred_element_type=F32)
            sel = (sc > thr) | ((sc == thr) & (rank <= eq_take))
            bias_ref[pl.ds(off, KEY_CHUNK), :] = jnp.where(sel, 0.0, MASK_BIAS).astype(BF16)
            return seen + jnp.sum(eq, axis=0, keepdims=True)

        lax.fori_loop(0, n_chunks, tie_body, jnp.zeros((1, tq), F32))

    acc_ref[...] = jnp.zeros(acc_ref.shape, F32)

    def qk(c, slot):
        off = pl.multiple_of(c * KEY_CHUNK, KEY_CHUNK)
        bias = bias_ref[pl.ds(off, KEY_CHUNK), :]
        for g in range(N_KV_HEADS):
            k_g = k_ref[pl.ds(off, KEY_CHUNK), g * HEAD_DIM:(g + 1) * HEAD_DIM]
            lg_ref[slot, g] = lax.dot_general(jnp.concatenate([k_g, bias], axis=1), qaug_ref[g], NT_DIMS,
                                              preferred_element_type=F32)

    def softmax_pv(c, slot, stats):
        off = pl.multiple_of(c * KEY_CHUNK, KEY_CHUNK)
        out = []
        for g in range(N_KV_HEADS):
            m_old, l_old = stats[2 * g], stats[2 * g + 1]
            logits = lg_ref[slot, g]
            m_new = jnp.maximum(m_old, jnp.max(logits, axis=0, keepdims=True))
            p = jnp.exp2(logits - m_new)
            alpha = jnp.exp2(m_old - m_new)
            l_new = alpha * l_old + jnp.sum(p, axis=0, keepdims=True)
            pv = jnp.dot(vt_ref[g, :, pl.ds(off, KEY_CHUNK)], p.astype(BF16), preferred_element_type=F32)
            acc_ref[g] = alpha * acc_ref[g] + pv
            out += [m_new, l_new]
        return tuple(out)

    last = n_chunks - 1
    qk(0, 0)

    def attn_body(i, stats):
        c = 2 * i
        qk(jnp.minimum(c + 1, last), 1)
        stats = softmax_pv(c, 0, stats)

        def second_half(stats):
            qk(jnp.minimum(c + 2, last), 0)
            return softmax_pv(c + 1, 1, stats)

        return lax.cond(c + 1 < n_chunks, second_half, lambda s: s, stats)

    init = (jnp.full((1, rep * tq), NEG_INF, F32), jnp.zeros((1, rep * tq), F32)) * N_KV_HEADS
    stats = lax.fori_loop(0, (n_chunks + 1) // 2, attn_body, init)

    for g in range(N_KV_HEADS):
        o_t = acc_ref[g] / stats[2 * g + 1]
        for r in range(rep):
            hd = g * rep + r
            o_ref[:, hd * HEAD_DIM:(hd + 1) * HEAD_DIM] = o_t[:, r * tq:(r + 1) * tq].T.astype(o_ref.dtype)


def _dsa(q, qs, w_t, kk, k, v_t, seq_len, top_k):
    bsz, lp, dq = q.shape
    lk = kk.shape[1]
    rep = N_HEADS // N_KV_HEADS
    qrow = lambda w: pl.BlockSpec((None, Q_TILE, w), lambda b, j: (b, j, 0))
    keys = lambda w: pl.BlockSpec((None, lk, w), lambda b, j: (b, 0, 0))
    kern = functools.partial(_dsa_kernel, seq_len=seq_len, top_k=top_k)
    return pl.pallas_call(
        kern,
        grid=(bsz, lp // Q_TILE),
        in_specs=[qrow(dq), qrow(qs.shape[-1]),
                  pl.BlockSpec((None, N_IDX_HEADS, Q_TILE), lambda b, j: (b, 0, j)),
                  keys(IDX_W), keys(k.shape[-1]),
                  pl.BlockSpec((None, N_KV_HEADS, HEAD_DIM, lk), lambda b, j: (b, 0, 0, 0))],
        out_specs=qrow(dq),
        out_shape=jax.ShapeDtypeStruct((bsz, lp, dq), BF16),
        scratch_shapes=[pltpu.VMEM((lk, Q_TILE), F32), pltpu.VMEM((lk, Q_TILE), BF16),
                        pltpu.VMEM((N_IDX_HEADS * Q_TILE, IDX_W), BF16),
                        pltpu.VMEM((N_KV_HEADS, rep * Q_TILE, 2 * HEAD_DIM), BF16),
                        pltpu.VMEM((N_KV_HEADS, HEAD_DIM, rep * Q_TILE), F32),
                        pltpu.VMEM((2, N_KV_HEADS, KEY_CHUNK, rep * Q_TILE), F32)],
        compiler_params=pltpu.CompilerParams(
            dimension_semantics=("parallel", "arbitrary"), vmem_limit_bytes=VMEM_LIMIT),
        name="dsa_attention",
    )(q, qs, w_t, kk, k, v_t)


def _merge_ffn_kernel(s_ref, ya_ref, ao_ref, gate_ref, wo_ref, wm_ref, fng_ref, up_ref, dww_ref,
                      dwb_ref, down_ref, fin_ref, out_ref, carry_ref, ubuf_ref, hb_ref, acc_ref,
                      obuf_ref, out_sem, *, seq_len, n_b, n_t):
    tm, d = s_ref.shape
    d_ff = down_ref.shape[0]
    b, i = pl.program_id(0), pl.program_id(1)

    @pl.when(i == 0)
    def _():
        carry_ref[...] = jnp.zeros(carry_ref.shape, F32)

    y_b = jnp.dot(ao_ref[...], wo_ref[...], preferred_element_type=F32)
    mix = gate_ref[:, :d].astype(F32) * ya_ref[...].astype(F32) + gate_ref[:, d:].astype(F32) * y_b
    stream = s_ref[...] + jnp.dot(mix.astype(BF16), wm_ref[...], preferred_element_type=F32)
    h = stream * lax.rsqrt(jnp.mean(stream * stream, axis=-1, keepdims=True) + EPS) * fng_ref[...]
    hb_ref[...] = h.astype(BF16)
    acc_ref[...] = stream

    def conv3(u, col, slot):
        ubuf_ref[slot, 0:SUBLANES, :] = carry_ref[:, pl.ds(col, FFN_CHUNK)]
        ubuf_ref[slot, SUBLANES:, :] = u
        carry_ref[:, pl.ds(col, FFN_CHUNK)] = u[tm - SUBLANES:, :]
        w = dww_ref[:, pl.ds(col, FFN_CHUNK)]
        out = dwb_ref[:, pl.ds(col, FFN_CHUNK)] + w[2:3, :] * u
        out = out + w[1:2, :] * ubuf_ref[slot, SUBLANES - 1:SUBLANES - 1 + tm, :]
        return out + w[0:1, :] * ubuf_ref[slot, SUBLANES - 2:SUBLANES - 2 + tm, :]

    def chunk(c, carry):
        col_a = pl.multiple_of(c * FFN_CHUNK, FFN_CHUNK)
        col_b = pl.multiple_of(d_ff + c * FFN_CHUNK, FFN_CHUNK)
        hb = hb_ref[...]
        a = conv3(jnp.dot(hb, up_ref[:, pl.ds(col_a, FFN_CHUNK)], preferred_element_type=F32), col_a, 0)
        b = conv3(jnp.dot(hb, up_ref[:, pl.ds(col_b, FFN_CHUNK)], preferred_element_type=F32), col_b, 1)
        act = (a * jax.nn.sigmoid(a) * b).astype(BF16)
        acc_ref[...] += jnp.dot(act, down_ref[pl.ds(col_a, FFN_CHUNK), :], preferred_element_type=F32)
        return carry

    lax.fori_loop(0, d_ff // FFN_CHUNK, chunk, 0, unroll=True)

    step = b * n_t + i
    slot = step % 2
    y = acc_ref[...]
    obuf_ref[slot] = y * lax.rsqrt(jnp.mean(y * y, axis=-1, keepdims=True) + EPS) * fin_ref[...]

    def tile_copy(kind, bb, tile, buf):
        if kind == "first":
            src0, rows, dst0 = N_META, min(tm, seq_len) - N_META, 0
        elif kind == "last":
            src0, rows, dst0 = 0, seq_len - (n_t - 1) * tm, (n_t - 1) * tm - N_META
        else:
            src0, rows, dst0 = 0, tm, pl.multiple_of(tile * tm - N_META, SUBLANES)
        return pltpu.make_async_copy(obuf_ref.at[buf, pl.ds(src0, rows)],
                                     out_ref.at[bb, pl.ds(dst0, rows)], out_sem.at[buf])

    def per_kind(tile, fn):
        pl.when(tile == 0)(lambda: fn("first"))
        if n_t > 2:
            pl.when((tile > 0) & (tile < n_t - 1))(lambda: fn("middle"))
        if n_t > 1:
            pl.when(tile == n_t - 1)(lambda: fn("last"))

    per_kind(i, lambda kind: tile_copy(kind, b, i, slot).start())

    @pl.when(step > 0)
    def _():
        prev_tile = jnp.where(i == 0, n_t - 1, i - 1)
        prev_b = jnp.where(i == 0, b - 1, b)
        per_kind(prev_tile, lambda kind: tile_copy(kind, prev_b, prev_tile, 1 - slot).wait())

    @pl.when(step == n_b * n_t - 1)
    def _():
        per_kind(i, lambda kind: tile_copy(kind, b, i, slot).wait())


def _merge_ffn(stream, ya, ao, gates, wo, wm, fng, up, dww, dwb, down, fin, tm, seq_len):
    bsz, lp, d = stream.shape
    row = lambda w: pl.BlockSpec((None, tm, w), lambda b, i: (b, i, 0))
    full = lambda a: pl.BlockSpec(a.shape, lambda b, i: (0,) * a.ndim, pipeline_mode=pl.Buffered(1))
    return pl.pallas_call(
        functools.partial(_merge_ffn_kernel, seq_len=seq_len, n_b=bsz, n_t=lp // tm),
        grid=(bsz, lp // tm),
        in_specs=[row(d), row(d), row(d), row(2 * d), full(wo), full(wm), full(fng), full(up),
                  full(dww), full(dwb), full(down), full(fin)],
        out_specs=pl.BlockSpec(memory_space=pl.ANY),
        out_shape=jax.ShapeDtypeStruct((bsz, seq_len - N_META, d), F32),
        scratch_shapes=[pltpu.VMEM((SUBLANES, up.shape[1]), F32),
                        pltpu.VMEM((2, SUBLANES + tm, FFN_CHUNK), F32),
                        pltpu.VMEM((tm, d), BF16), pltpu.VMEM((tm, d), F32),
                        pltpu.VMEM((2, tm, d), F32), pltpu.SemaphoreType.DMA((2,))],
        compiler_params=pltpu.CompilerParams(
            dimension_semantics=("arbitrary", "arbitrary"), vmem_limit_bytes=VMEM_LIMIT),
        name="merge_ffn",
    )(stream, ya, ao, gates, wo, wm, fng, up, dww, dwb, down, fin)


def _row_tile(lp, prefer=ROW_TILE):
    for t in (prefer, ROW_TILE, 256, LANES):
        if lp % t == 0:
            return t
    raise ValueError(f"padded length {lp} is not a multiple of {LANES}")


def kernel(x, meta_tokens, mix_norm_g, w_in, conv_ln_g, conv_ln_b, conv_dw_w, conv_dw_b, conv_pw_out,
           attn_w_o, w_merge_out, ffn_norm_g, ffn_up, ffn_dw_w, ffn_dw_b, ffn_down, final_norm_g):
    bsz, seq, d = x.shape
    depth = w_in.shape[0]
    seq_len = seq + N_META
    lp = -(-seq_len // LANES) * LANES
    lk = -(-lp // KEY_CHUNK) * KEY_CHUNK
    tm = _row_tile(lp)
    top_k = min(TOPK_MAX, seq_len // 4)
    assert top_k > N_META and conv_dw_w.shape[1] == CONV_WIDTH and ffn_dw_w.shape[1] == FFN_CONV_WIDTH
    d_conv = conv_pw_out.shape[1]
    d_ff = ffn_down.shape[1]
    assert d_ff % FFN_CHUNK == 0 and d % LANES == 0

    meta = jnp.broadcast_to(meta_tokens.astype(x.dtype)[None], (bsz, N_META, d))
    stream = jnp.concatenate([meta, x, jnp.zeros((bsz, lp - seq_len, d), x.dtype)], axis=1)
    tabs = _rope_tables(lp, HEAD_DIM, 1) + _rope_tables(lp, IDX_DIM, LANES // IDX_DIM)

    n_q, n_kv = N_HEADS * HEAD_DIM, N_KV_HEADS * HEAD_DIM
    n_qi = N_IDX_HEADS * IDX_DIM
    edges = [0, 2 * d_conv]
    for wdt in (n_q, n_kv, n_kv, n_qi, IDX_DIM, N_IDX_HEADS, 2 * d):
        edges.append(edges[-1] + wdt)
    row2 = lambda a: a.reshape(1, -1).astype(F32)
    pad_keys = lambda a: jnp.pad(a, ((0, 0), (0, lk - lp), (0, 0)))

    for l in range(depth):
        w = w_in[l]
        wu = w[:, edges[0]:edges[1]].astype(BF16)
        wq = (w[:, edges[1]:edges[2]] * (HEAD_DIM ** -0.5 * math.log2(math.e))).astype(BF16)
        wkv = w[:, edges[2]:edges[4]].astype(BF16)
        wqi = w[:, edges[4]:edges[5]].astype(BF16)
        wkw = jnp.pad(w[:, edges[5]:edges[7]], ((0, 0), (0, LANES - IDX_DIM - N_IDX_HEADS))).astype(BF16)
        wg = w[:, edges[7]:edges[8]].astype(BF16)

        ya, q, k, v, qs, kk, w_idx, gates = _in_proj(
            stream, row2(mix_norm_g[l]), wu, wq, wkv, wqi, wkw, wg, conv_dw_w[l],
            row2(conv_dw_b[l]), row2(conv_ln_g[l]), row2(conv_ln_b[l]), conv_pw_out[l].astype(BF16), tabs, tm)
        v_t = jnp.swapaxes(pad_keys(v).reshape(bsz, lk, N_KV_HEADS, HEAD_DIM), 1, 3).swapaxes(1, 2)
        ao = _dsa(q, qs, jnp.swapaxes(w_idx, 1, 2), pad_keys(kk), pad_keys(k), v_t, seq_len, top_k)
        assert l == depth - 1, "merge_ffn applies the final norm and drops the meta rows: single layer only"
        out = _merge_ffn(stream, ya, ao, gates, attn_w_o[l].astype(BF16), w_merge_out[l].astype(BF16),
                         row2(ffn_norm_g[l]), ffn_up[l].astype(BF16), ffn_dw_w[l], row2(ffn_dw_b[l]),
                         ffn_down[l].astype(BF16), row2(final_norm_g), _row_tile(lp, FFN_ROW_TILE), seq_len)
    return out
```
